```python
import math
import jax, jax.numpy as jnp
from jax import lax
import numpy as np

D_MODEL = 1024
BATCH = 8
SEQ = 4096
DEPTH = 4

N_MIXERS = 3
D_FF = 2816
EPS = 1e-6
GDN_HEADS = 8
GDN_DK = 128
GDN_DV = 128
GDN_CONV = 4
GDN_CHUNK = 64
SC_WIDTH = 3
NSA_HEADS = 16
NSA_KV_HEADS = 4
NSA_GROUP = NSA_HEADS // NSA_KV_HEADS
NSA_DH = 64
CMP_BLOCK = 32
CMP_STRIDE = 16
CMP_HIDDEN = 256
SLC_BLOCK = 64
SLC_TOPK = 16
N_LOCAL = 2
WINDOW = 512
NSA_Q_BLOCK = 32
ROPE_THETA = 10000.0
NEG = -1e30
FORCE = 1e9
N_GDN = (DEPTH + 2) // 3
N_SC = (DEPTH + 1) // 3
N_NSA = DEPTH // 3

kernel_name = "hybrid_gdn_shortconv_nsa_macaron"


def rmsnorm(x, w):
    xf = x.astype(jnp.float32)
    y = xf * lax.rsqrt(jnp.mean(xf * xf, -1, keepdims=True) + EPS)
    return (y * w.astype(jnp.float32)).astype(x.dtype)


def l2norm(x):
    xf = x.astype(jnp.float32)
    return xf * lax.rsqrt(jnp.sum(xf * xf, -1, keepdims=True) + EPS)


def causal_depthwise_conv(x, w):
    width = w.shape[0]
    S = x.shape[1]
    xp = jnp.pad(x, ((0, 0), (width - 1, 0), (0, 0)))
    y = xp[:, 0:S] * w[0]
    for j in range(1, width):
        y = y + xp[:, j:j + S] * w[j]
    return y


def swiglu(h, w_gate_up, w_down):
    g, u = jnp.split(h @ w_gate_up, 2, axis=-1)
    return (jax.nn.silu(g) * u) @ w_down


def rope_tables(positions, dim):
    inv = 1.0 / (ROPE_THETA ** (jnp.arange(0, dim, 2, dtype=jnp.float32) / dim))
    ang = positions.astype(jnp.float32)[..., None] * inv
    ang = jnp.concatenate([ang, ang], axis=-1)[:, :, None, :]
    return jnp.cos(ang), jnp.sin(ang)


def apply_rope(x, cos, sin):
    x1, x2 = jnp.split(x, 2, axis=-1)
    rot = jnp.concatenate([-x2, x1], axis=-1)
    return (x * cos + rot * sin).astype(x.dtype)


def gated_delta_rule_chunked(q, k, v, g, beta):
    B, S, H, dk = q.shape
    dv = v.shape[-1]
    C = GDN_CHUNK
    N = S // C

    def to_chunks(t):
        return jnp.moveaxis(t.reshape((B, N, C, H) + t.shape[3:]), 3, 1)

    q = to_chunks(q) * (dk ** -0.5)
    k = to_chunks(k)
    v = to_chunks(v)
    g = to_chunks(g)
    beta = to_chunks(beta)
    gc = jnp.cumsum(g, axis=-1)
    causal = jnp.tril(jnp.ones((C, C), bool))
    strict = jnp.tril(jnp.ones((C, C), bool), -1)
    decay = jnp.exp(jnp.where(causal, gc[..., :, None] - gc[..., None, :], -jnp.inf))
    k_beta = k * beta[..., None]
    A = jnp.where(strict, jnp.einsum('bhncd,bhned->bhnce', k_beta, k) * decay, 0.0)
    eye = jnp.eye(C, dtype=jnp.float32)
    u = lax.linalg.triangular_solve(eye + A, v * beta[..., None], left_side=True,
                                    lower=True, unit_diagonal=True)
    w = lax.linalg.triangular_solve(eye + A, k_beta * jnp.exp(gc)[..., None], left_side=True,
                                    lower=True, unit_diagonal=True)
    attn_intra = jnp.where(causal, jnp.einsum('bhncd,bhned->bhnce', q, k) * decay, 0.0)
    q_dec = q * jnp.exp(gc)[..., None]
    k_dec = k * jnp.exp(gc[..., -1:] - gc)[..., None]
    g_last = jnp.exp(gc[..., -1])
    xs = tuple(jnp.moveaxis(t, 2, 0) for t in (q_dec, k_dec, u, w, attn_intra, g_last))

    def step(state, inp):
        qd, kd, u_c, w_c, a_c, gl = inp
        v_new = u_c - jnp.einsum('bhcd,bhde->bhce', w_c, state)
        o = jnp.einsum('bhcd,bhde->bhce', qd, state) + jnp.einsum('bhce,bhef->bhcf', a_c, v_new)
        state = state * gl[..., None, None] + jnp.einsum('bhcd,bhce->bhde', kd, v_new)
        return state, o

    s0 = jnp.zeros((B, H, dk, dv), jnp.float32)
    _, o = lax.scan(step, s0, xs)
    return jnp.transpose(o, (1, 0, 3, 2, 4)).reshape(B, S, H, dv)


def gdn_mixer(h, w_in, conv_w, A_log, dt_bias, out_norm, w_out):
    B, S, _ = h.shape
    H, dk, dv = GDN_HEADS, GDN_DK, GDN_DV
    n_qkv = 2 * H * dk + H * dv
    proj = h @ w_in
    qkv, gate, a, b = jnp.split(proj, [n_qkv, n_qkv + H * dv, n_qkv + H * dv + H], axis=-1)
    qkv = jax.nn.silu(causal_depthwise_conv(qkv, conv_w))
    q, k, v = jnp.split(qkv, [H * dk, 2 * H * dk], axis=-1)
    q = l2norm(q.reshape(B, S, H, dk))
    k = l2norm(k.reshape(B, S, H, dk))
    v = v.reshape(B, S, H, dv).astype(jnp.float32)
    beta = jax.nn.sigmoid(b.astype(jnp.float32))
    g = -jnp.exp(A_log.astype(jnp.float32)) * jax.nn.softplus(
        a.astype(jnp.float32) + dt_bias.astype(jnp.float32))
    o = gated_delta_rule_chunked(q, k, v, g, beta)
    o = rmsnorm(o, out_norm) * jax.nn.silu(gate.reshape(B, S, H, dv).astype(jnp.float32))
    return o.reshape(B, S, H * dv).astype(h.dtype) @ w_out


def short_conv_mixer(h, w_in, conv_w, w_out):
    b_gate, c_gate, xin = jnp.split(h @ w_in, 3, axis=-1)
    y = causal_depthwise_conv(c_gate * xin, conv_w)
    return (b_gate * y) @ w_out


def nsa_mixer(h, cos, sin, w_in, q_norm, k_norm, cmp_pe, cmp_w1, cmp_b1, cmp_w2, cmp_b2, w_out):
    B, S, _ = h.shape
    H, Hk, G, dh = NSA_HEADS, NSA_KV_HEADS, NSA_GROUP, NSA_DH
    kvw = Hk * dh
    sizes = [H * dh] + [kvw] * 6
    splits = [int(s) for s in np.cumsum(sizes)]
    q, kc, vc, ks, vs, kw, vw, gates = jnp.split(h @ w_in, splits, axis=-1)
    q = rmsnorm(q.reshape(B, S, H, dh), q_norm)
    q_rot = apply_rope(q, cos, sin)
    ks = apply_rope(rmsnorm(ks.reshape(B, S, Hk, dh), k_norm[1]), cos, sin)
    kw = apply_rope(rmsnorm(kw.reshape(B, S, Hk, dh), k_norm[2]), cos, sin)
    vs = vs.reshape(B, S, Hk, dh)
    vw = vw.reshape(B, S, Hk, dh)
    gates = jax.nn.sigmoid(gates.reshape(B, S, H, 3).astype(jnp.float32))

    r = CMP_BLOCK // CMP_STRIDE
    n_chunks = S // CMP_STRIDE
    Nc = n_chunks - r + 1

    def compress(t, i):
        c = t.reshape(B, n_chunks, CMP_STRIDE, Hk, dh)
        blocks = jnp.concatenate([c[:, j:j + Nc] for j in range(r)], axis=2)
        blocks = blocks + cmp_pe[i][:, None, :]
        blocks = jnp.moveaxis(blocks, 3, 2).reshape(B, Nc, Hk, CMP_BLOCK * dh)
        hid = jax.nn.gelu(blocks @ cmp_w1[i] + cmp_b1[i])
        return hid @ cmp_w2[i] + cmp_b2[i]

    kc = rmsnorm(compress(kc.reshape(B, S, Hk, dh), 0), k_norm[0])
    vc = compress(vc.reshape(B, S, Hk, dh), 1)

    qn_t = q.reshape(B, S, Hk, G, dh).transpose(0, 2, 3, 1, 4)
    qr_t = q_rot.reshape(B, S, Hk, G, dh).transpose(0, 2, 3, 1, 4)
    kc_t = kc.transpose(0, 2, 1, 3)
    vc_t = vc.transpose(0, 2, 1, 3)
    Ns = S // SLC_BLOCK
    n_sel = min(SLC_TOPK, Ns)
    ks_blk = ks.transpose(0, 2, 1, 3).reshape(B, Hk, Ns, SLC_BLOCK, dh)
    vs_blk = vs.transpose(0, 2, 1, 3).reshape(B, Hk, Ns, SLC_BLOCK, dh)
    pad = ((0, 0), (0, 0), (WINDOW, 0), (0, 0))
    kw_pad = jnp.pad(kw.transpose(0, 2, 1, 3), pad)
    vw_pad = jnp.pad(vw.transpose(0, 2, 1, 3), pad)

    cmp_end = jnp.arange(Nc) * CMP_STRIDE + CMP_BLOCK - 1
    ci = jnp.arange(Nc)[:, None]
    sj = jnp.arange(Ns)[None, :]
    overlap = jnp.clip(jnp.minimum(ci * CMP_STRIDE + CMP_BLOCK, (sj + 1) * SLC_BLOCK)
                       - jnp.maximum(ci * CMP_STRIDE, sj * SLC_BLOCK), 0, None)
    overlap = overlap.astype(jnp.float32) / CMP_BLOCK
    blk_ids = jnp.arange(Ns)
    b_idx = jnp.arange(B)[:, None, None, None]
    h_idx = jnp.arange(Hk)[None, :, None, None]
    scale = dh ** -0.5
    QB = NSA_Q_BLOCK

    def block(i):
        s0 = i * QB
        tpos = s0 + jnp.arange(QB)
        qn = lax.dynamic_slice_in_dim(qn_t, s0, QB, axis=3).astype(jnp.float32)
        qr = lax.dynamic_slice_in_dim(qr_t, s0, QB, axis=3).astype(jnp.float32)
        sc = jnp.einsum('bhgqd,bhkd->bhgqk', qn, kc_t) * scale
        cmask = cmp_end[None, :] <= tpos[:, None]
        p_c = jax.nn.softmax(jnp.where(cmask, sc, NEG), axis=-1)
        p_c = jnp.where(cmask.any(-1)[:, None], p_c, 0.0)
        o_c = jnp.einsum('bhgqk,bhkd->bhgqd', p_c, vc_t)
        imp = jnp.einsum('bhgqc,cs->bhqs', p_c, overlap)
        svalid = (blk_ids * SLC_BLOCK)[None, :] <= tpos[:, None]
        dist = (tpos // SLC_BLOCK)[:, None] - blk_ids[None, :]
        forced = (blk_ids == 0)[None, :] | ((dist >= 0) & (dist < N_LOCAL))
        score = jnp.where(svalid & forced, FORCE, jnp.where(svalid, imp, -1.0))
        top_score, idx = lax.top_k(score, n_sel)
        sel_valid = top_score >= 0.0
        kb = ks_blk[b_idx, h_idx, idx]
        vb = vs_blk[b_idx, h_idx, idx]
        ss = jnp.einsum('bhgqd,bhqnkd->bhgqnk', qr, kb) * scale
        kpos = idx[..., None] * SLC_BLOCK + jnp.arange(SLC_BLOCK)
        smask = sel_valid[..., None] & (kpos <= tpos[:, None, None])
        ss = jnp.where(smask[:, :, None], ss, NEG)
        p_s = jax.nn.softmax(ss.reshape(B, Hk, G, QB, n_sel * SLC_BLOCK), axis=-1)
        p_s = p_s.reshape(B, Hk, G, QB, n_sel, SLC_BLOCK)
        o_s = jnp.einsum('bhgqnk,bhqnkd->bhgqd', p_s, vb)
        kwin = lax.dynamic_slice_in_dim(kw_pad, s0, WINDOW + QB, axis=2)
        vwin = lax.dynamic_slice_in_dim(vw_pad, s0, WINDOW + QB, axis=2)
        wpos = s0 - WINDOW + jnp.arange(WINDOW + QB)
        wmask = ((wpos[None, :] <= tpos[:, None]) & (wpos[None, :] > tpos[:, None] - WINDOW)
                 & (wpos[None, :] >= 0))
        sw = jnp.einsum('bhgqd,bhkd->bhgqk', qr, kwin) * scale
        p_w = jax.nn.softmax(jnp.where(wmask, sw, NEG), axis=-1)
        o_w = jnp.einsum('bhgqk,bhkd->bhgqd', p_w, vwin)
        return o_c, o_s, o_w

    o_c, o_s, o_w = lax.map(block, jnp.arange(S // QB))

    def to_bshd(o):
        return jnp.transpose(o, (1, 0, 4, 2, 3, 5)).reshape(B, S, H, dh)

    o = (gates[..., 0:1] * to_bshd(o_c) + gates[..., 1:2] * to_bshd(o_s)
         + gates[..., 2:3] * to_bshd(o_w))
    return o.reshape(B, S, H * dh).astype(h.dtype) @ w_out


def setup_inputs(seed: int = 0) -> dict:
    key = jax.random.key(seed)
    ks = jax.random.split(key, 26)
    f32 = jnp.float32
    D, F = D_MODEL, D_FF

    def nrm(k, shape, scale):
        return jax.random.normal(k, shape, f32) * scale

    def gain(k, shape):
        return 1.0 + 0.02 * jax.random.normal(k, shape, f32)

    gdn_in = 2 * GDN_HEADS * GDN_DK + 2 * GDN_HEADS * GDN_DV + 2 * GDN_HEADS
    nsa_in = NSA_HEADS * NSA_DH + 6 * NSA_KV_HEADS * NSA_DH + 3 * NSA_HEADS
    n_conv = 2 * GDN_HEADS * GDN_DK + GDN_HEADS * GDN_DV
    dt = jnp.exp(jax.random.uniform(ks[7], (N_GDN, GDN_HEADS), f32,
                                    minval=math.log(1e-3), maxval=math.log(1e-1)))
    return {
        "x": jax.random.normal(ks[0], (BATCH, SEQ, D), f32),
        "positions": jnp.broadcast_to(jnp.arange(SEQ, dtype=jnp.int32), (BATCH, SEQ)),
        "ffn_norm": gain(ks[1], (DEPTH, 2, D)),
        "ffn_w_gate_up": nrm(ks[2], (DEPTH, 2, D, 2 * F), D ** -0.5),
        "ffn_w_down": nrm(ks[3], (DEPTH, 2, F, D), F ** -0.5),
        "mixer_norm": gain(ks[4], (DEPTH, D)),
        "gdn_w_in": nrm(ks[5], (N_GDN, D, gdn_in), D ** -0.5),
        "gdn_conv_w": nrm(ks[6], (N_GDN, GDN_CONV, n_conv), GDN_CONV ** -0.5),
        "gdn_A_log": jnp.log(jax.random.uniform(ks[8], (N_GDN, GDN_HEADS), f32, minval=1.0, maxval=16.0)),
        "gdn_dt_bias": dt + jnp.log(-jnp.expm1(-dt)),
        "gdn_out_norm": gain(ks[9], (N_GDN, GDN_DV)),
        "gdn_w_out": nrm(ks[10], (N_GDN, GDN_HEADS * GDN_DV, D), (GDN_HEADS * GDN_DV) ** -0.5),
        "sc_w_in": nrm(ks[11], (N_SC, D, 3 * D), D ** -0.5),
        "sc_conv_w": nrm(ks[12], (N_SC, SC_WIDTH, D), SC_WIDTH ** -0.5),
        "sc_w_out": nrm(ks[13], (N_SC, D, D), D ** -0.5),
        "nsa_w_in": nrm(ks[14], (N_NSA, D, nsa_in), D ** -0.5),
        "nsa_q_norm": gain(ks[15], (N_NSA, NSA_DH)),
        "nsa_k_norm": gain(ks[16], (N_NSA, 3, NSA_DH)),
        "nsa_cmp_pe": nrm(ks[17], (N_NSA, 2, CMP_BLOCK, NSA_DH), 0.02),
        "nsa_cmp_w1": nrm(ks[18], (N_NSA, 2, CMP_BLOCK * NSA_DH, CMP_HIDDEN), (CMP_BLOCK * NSA_DH) ** -0.5),
        "nsa_cmp_b1": nrm(ks[19], (N_NSA, 2, CMP_HIDDEN), 0.01),
        "nsa_cmp_w2": nrm(ks[20], (N_NSA, 2, CMP_HIDDEN, NSA_DH), CMP_HIDDEN ** -0.5),
        "nsa_cmp_b2": nrm(ks[21], (N_NSA, 2, NSA_DH), 0.01),
        "nsa_w_out": nrm(ks[22], (N_NSA, NSA_HEADS * NSA_DH, D), (NSA_HEADS * NSA_DH) ** -0.5),
    }


def reference(x, positions, ffn_norm, ffn_w_gate_up, ffn_w_down, mixer_norm,
              gdn_w_in, gdn_conv_w, gdn_A_log, gdn_dt_bias, gdn_out_norm, gdn_w_out,
              sc_w_in, sc_conv_w, sc_w_out,
              nsa_w_in, nsa_q_norm, nsa_k_norm, nsa_cmp_pe, nsa_cmp_w1, nsa_cmp_b1,
              nsa_cmp_w2, nsa_cmp_b2, nsa_w_out):
    cos, sin = rope_tables(positions, NSA_DH)
    h = x
    for layer in range(DEPTH):
        h = h + 0.5 * swiglu(rmsnorm(h, ffn_norm[layer, 0]), ffn_w_gate_up[layer, 0], ffn_w_down[layer, 0])
        hn = rmsnorm(h, mixer_norm[layer])
        kind = layer % N_MIXERS
        j = layer // N_MIXERS
        if kind == 0:
            mix = gdn_mixer(hn, gdn_w_in[j], gdn_conv_w[j], gdn_A_log[j], gdn_dt_bias[j],
                            gdn_out_norm[j], gdn_w_out[j])
        elif kind == 1:
            mix = short_conv_mixer(hn, sc_w_in[j], sc_conv_w[j], sc_w_out[j])
        else:
            mix = nsa_mixer(hn, cos, sin, nsa_w_in[j], nsa_q_norm[j], nsa_k_norm[j], nsa_cmp_pe[j],
                            nsa_cmp_w1[j], nsa_cmp_b1[j], nsa_cmp_w2[j], nsa_cmp_b2[j], nsa_w_out[j])
        h = h + mix
        h = h + 0.5 * swiglu(rmsnorm(h, ffn_norm[layer, 1]), ffn_w_gate_up[layer, 1], ffn_w_down[layer, 1])
    return h
```

```python
import functools

import numpy as np
import jax
import jax.numpy as jnp
from jax import lax
from jax.experimental import pallas as pl
from jax.experimental.pallas import tpu as pltpu

F32 = jnp.float32
BF16 = jnp.bfloat16

EPS = 1e-6
NEG = -1e30
FORCE = 1e9
S_FAR = 1 << 30
LANES = 128

GDN_HEADS = 8
GDN_DK = 128
GDN_DV = 128
GDN_CONV = 4
GDN_CHUNK = 64
SC_WIDTH = 3
NSA_HEADS = 16
NSA_KV_HEADS = 4
NSA_GROUP = NSA_HEADS // NSA_KV_HEADS
NSA_DH = 64
CMP_BLOCK = 32
CMP_STRIDE = 16
CMP_HIDDEN = 256
SLC_BLOCK = 64
SLC_TOPK = 16
N_LOCAL = 2
WINDOW = 512
ROPE_THETA = 10000.0
N_MIXERS = 3

VMEM_LIMIT = 56 * 1024 * 1024


def _cparams(sem):
    return pltpu.CompilerParams(dimension_semantics=sem, vmem_limit_bytes=VMEM_LIMIT)


def _resident(shape):
    nd = len(shape)
    return pl.BlockSpec(shape, lambda *_: (0,) * nd, pipeline_mode=pl.Buffered(1))


def _rms(x, w):
    return x * lax.rsqrt(jnp.mean(x * x, axis=-1, keepdims=True) + EPS) * w


def _silu(x):
    return x * jax.nn.sigmoid(x)


def _softplus(x):
    return jnp.maximum(x, 0.0) + jnp.log1p(jnp.exp(-jnp.abs(x)))


def _dot(a, b):
    return jnp.dot(a, b, preferred_element_type=F32)


def _dot_nt(a, b):
    return lax.dot_general(a, b, (((1,), (1,)), ((), ())), preferred_element_type=F32)


def _split(x):
    hi = x.astype(BF16)
    lo = (x - hi.astype(F32)).astype(BF16)
    return hi, lo


def _dot3(a, b):
    ah, al = _split(a)
    bh, bl = _split(b)
    return _dot(ah, bh) + (_dot(ah, bl) + _dot(al, bh))


def _dot2_lhs(a, b_bf16):
    ah, al = _split(a)
    return _dot(ah, b_bf16) + _dot(al, b_bf16)


def _ffn_kernel(x_ref, nw_ref, wg_ref, wu_ref, wd_ref, o_ref, acc_ref, *, nc):
    x = x_ref[...]
    xn = _rms(x, nw_ref[...]).astype(BF16)
    acc_ref[...] = jnp.zeros_like(acc_ref)

    def body(c, carry):
        g = _dot(xn, wg_ref[c])
        u = _dot(xn, wu_ref[c])
        a = (_silu(g) * u).astype(BF16)
        acc_ref[...] += _dot(a, wd_ref[c])
        return carry

    lax.fori_loop(0, nc, body, 0)
    o_ref[...] = x + 0.5 * acc_ref[...]


def _ffn(h2, nw, wg, wu, wd, tm=512):
    T, D = h2.shape
    nc = wg.shape[0]
    return pl.pallas_call(
        functools.partial(_ffn_kernel, nc=nc),
        grid=(T // tm,),
        in_specs=[pl.BlockSpec((tm, D), lambda i: (i, 0)),
                  _resident(nw.shape), _resident(wg.shape), _resident(wu.shape), _resident(wd.shape)],
        out_specs=pl.BlockSpec((tm, D), lambda i: (i, 0)),
        out_shape=jax.ShapeDtypeStruct((T, D), F32),
        scratch_shapes=[pltpu.VMEM((tm, D), F32)],
        compiler_params=_cparams(("parallel",)),
        name="ffn",
    )(h2, nw, wg, wu, wd)


def _prep_ffn_weights(w_gate_up, w_down, tf=256):
    D, F2 = w_gate_up.shape
    Fh = F2 // 2
    nc = Fh // tf
    wg = w_gate_up[:, :Fh].reshape(D, nc, tf).transpose(1, 0, 2).astype(BF16)
    wu = w_gate_up[:, Fh:].reshape(D, nc, tf).transpose(1, 0, 2).astype(BF16)
    wd = w_down.reshape(nc, tf, D).astype(BF16)
    return wg, wu, wd


def _norm_matmul_kernel(x_ref, nw_ref, w_ref, o_ref, *, tn):
    xn = _rms(x_ref[...], nw_ref[...]).astype(BF16)
    for c in range(w_ref.shape[1] // tn):
        o_ref[:, c * tn:(c + 1) * tn] = _dot(xn, w_ref[:, c * tn:(c + 1) * tn])


def _norm_matmul(h2, nw, w, tn, tm=256):
    T, D = h2.shape
    N = w.shape[1]
    return pl.pallas_call(
        functools.partial(_norm_matmul_kernel, tn=tn),
        grid=(T // tm,),
        in_specs=[pl.BlockSpec((tm, D), lambda i: (i, 0)), _resident(nw.shape), _resident(w.shape)],
        out_specs=pl.BlockSpec((tm, N), lambda i: (i, 0)),
        out_shape=jax.ShapeDtypeStruct((T, N), F32),
        compiler_params=_cparams(("parallel",)),
        name="norm_matmul",
    )(h2, nw, w)


def _matmul_res_kernel(a_ref, w_ref, h_ref, o_ref):
    o_ref[...] = h_ref[...] + _dot(a_ref[...], w_ref[...])


def _matmul_res(a, w, h2, tm=512):
    T, D = h2.shape
    K = a.shape[1]
    return pl.pallas_call(
        _matmul_res_kernel,
        grid=(T // tm,),
        in_specs=[pl.BlockSpec((tm, K), lambda i: (i, 0)), _resident(w.shape),
                  pl.BlockSpec((tm, D), lambda i: (i, 0))],
        out_specs=pl.BlockSpec((tm, D), lambda i: (i, 0)),
        out_shape=jax.ShapeDtypeStruct((T, D), F32),
        compiler_params=_cparams(("parallel",)),
        name="matmul_res",
    )(a, w, h2)


def _unit_lower_inverse(a, eye):
    x = -a
    t = eye + x
    n = a.shape[0]
    p = 2
    while p < n:
        x = _dot3(x, x)
        t = t + _dot3(t, x)
        p *= 2
    return t


def _gdn_kernel(qkv_ref, gate_ref, ab_ref, cw_ref, alog_ref, dtb_ref, onw_ref, o_ref, xbuf_ref, s_ref):
    C = GDN_CHUNK
    H, DK, DV = GDN_HEADS, GDN_DK, GDN_DV
    n = pl.program_id(1)

    @pl.when(n == 0)
    def _():
        xbuf_ref[0:8, :] = jnp.zeros((8, xbuf_ref.shape[1]), F32)
        s_ref[...] = jnp.zeros_like(s_ref)

    raw = qkv_ref[...]
    xbuf_ref[8:8 + C, :] = raw
    cw = cw_ref[...]
    y = (xbuf_ref[5:5 + C, :] * cw[0:1] + xbuf_ref[6:6 + C, :] * cw[1:2]
         + xbuf_ref[7:7 + C, :] * cw[2:3] + raw * cw[3:4])
    xbuf_ref[0:8, :] = raw[C - 8:C]
    y = _silu(y)

    ab = ab_ref[...]
    g_all = -jnp.exp(alog_ref[...]) * _softplus(ab + dtb_ref[...])
    beta_all = jax.nn.sigmoid(ab)
    row = lax.broadcasted_iota(jnp.int32, (C, LANES), 0)
    gc = g_all
    s = 1
    while s < C:
        gc = gc + jnp.where(row >= s, pltpu.roll(gc, s, axis=0), 0.0)
        s *= 2
    gc_t = gc.T

    ri = lax.broadcasted_iota(jnp.int32, (C, C), 0)
    ci = lax.broadcasted_iota(jnp.int32, (C, C), 1)
    lower = ri >= ci
    strict = ri > ci
    eye = (ri == ci).astype(F32)

    for h in range(H):
        gcol = gc[:, h:h + 1]
        grow = gc_t[h:h + 1, :]
        beta = beta_all[:, H + h:H + h + 1]
        dec = jnp.where(lower, jnp.exp(jnp.where(lower, gcol - grow, 0.0)), 0.0)
        q = y[:, h * DK:(h + 1) * DK]
        k = y[:, H * DK + h * DK:H * DK + (h + 1) * DK]
        v = y[:, 2 * H * DK + h * DV:2 * H * DK + (h + 1) * DV]
        qn = q * lax.rsqrt(jnp.sum(q * q, -1, keepdims=True) + EPS) * (DK ** -0.5)
        kn = k * lax.rsqrt(jnp.sum(k * k, -1, keepdims=True) + EPS)
        kb = kn * beta
        knb = kn.astype(BF16)
        a_mat = jnp.where(strict, _dot_nt(kb.astype(BF16), knb) * dec, 0.0)
        attn = jnp.where(lower, _dot_nt(qn.astype(BF16), knb) * dec, 0.0)
        t_inv = _unit_lower_inverse(a_mat, eye)
        egc = jnp.exp(gcol)
        uw = _dot3(t_inv, jnp.concatenate([v * beta, kb * egc], axis=1))
        u = uw[:, :DV]
        w = uw[:, DV:]
        st = s_ref[h]
        r = _dot(jnp.concatenate([w, qn * egc], axis=0).astype(BF16), st.astype(BF16))
        v_new = u - r[:C]
        vb = v_new.astype(BF16)
        o = r[C:] + _dot(attn.astype(BF16), vb)
        glast = gc[C - 1:C, h:h + 1]
        kdec = kn * jnp.exp(glast - gcol)
        s_ref[h] = st * jnp.exp(glast) + _dot(kdec.T.astype(BF16), vb)
        on = _rms(o, onw_ref[...])
        gt = gate_ref[:, h * DV:(h + 1) * DV]
        o_ref[:, h * DV:(h + 1) * DV] = (on * _silu(gt)).astype(BF16)


def _gdn_core(proj, cw, alog, dtb, onw, B, S):
    C = GDN_CHUNK
    NQ = 2 * GDN_HEADS * GDN_DK + GDN_HEADS * GDN_DV
    NG = GDN_HEADS * GDN_DV
    nchunk = S // C
    T = B * S
    return pl.pallas_call(
        _gdn_kernel,
        grid=(B, nchunk),
        in_specs=[pl.BlockSpec((C, NQ), lambda b, n: (b * nchunk + n, 0)),
                  pl.BlockSpec((C, NG), lambda b, n: (b * nchunk + n, NQ // NG)),
                  pl.BlockSpec((C, LANES), lambda b, n: (b * nchunk + n, (NQ + NG) // LANES)),
                  _resident(cw.shape), _resident(alog.shape), _resident(dtb.shape), _resident(onw.shape)],
        out_specs=pl.BlockSpec((C, NG), lambda b, n: (b * nchunk + n, 0)),
        out_shape=jax.ShapeDtypeStruct((T, NG), BF16),
        scratch_shapes=[pltpu.VMEM((8 + C, NQ), F32), pltpu.VMEM((GDN_HEADS, GDN_DK, GDN_DV), F32)],
        compiler_params=_cparams(("parallel", "arbitrary")),
        name="gdn_core",
    )(proj, proj, proj, cw, alog, dtb, onw)


def _gdn_mixer(h2, nw, w_in, conv_w, a_log, dt_bias, out_norm, w_out, B, S):
    H = GDN_HEADS
    n_used = w_in.shape[1]
    n_pad = -n_used % LANES
    w_in_p = jnp.pad(w_in, ((0, 0), (0, n_pad))).astype(BF16)
    proj = _norm_matmul(h2, nw, w_in_p, tn=w_in_p.shape[1] // 3)
    lane_pad = LANES - H
    alog = jnp.pad(a_log, (0, lane_pad)).reshape(1, LANES)
    dtb = jnp.pad(dt_bias, (0, lane_pad)).reshape(1, LANES)
    o = _gdn_core(proj, conv_w, alog, dtb, out_norm.reshape(1, GDN_DV), B, S)
    return _matmul_res(o, w_out.astype(BF16), h2)


def _sc_kernel(h_ref, nw_ref, win_ref, cw_ref, wout_ref, o_ref, cx_ref):
    tm, D = h_ref.shape
    i = pl.program_id(1)

    @pl.when(i == 0)
    def _():
        cx_ref[0:8, :] = jnp.zeros((8, D), F32)

    x = h_ref[...]
    xn = _rms(x, nw_ref[...]).astype(BF16)
    bg = _dot(xn, win_ref[:, 0:D])
    cx = _dot(xn, win_ref[:, D:2 * D]) * _dot(xn, win_ref[:, 2 * D:3 * D])
    cx_ref[8:8 + tm, :] = cx
    cw = cw_ref[...]
    y = cx_ref[6:6 + tm, :] * cw[0:1] + cx_ref[7:7 + tm, :] * cw[1:2] + cx * cw[2:3]
    cx_ref[0:8, :] = cx[tm - 8:tm]
    o_ref[...] = x + _dot((bg * y).astype(BF16), wout_ref[...])


def _sc_mixer(h2, nw, w_in, conv_w, w_out, B, S, tm=256):
    T, D = h2.shape
    nt = S // tm
    return pl.pallas_call(
        _sc_kernel,
        grid=(B, nt),
        in_specs=[pl.BlockSpec((tm, D), lambda b, i: (b * nt + i, 0)),
                  _resident(nw.shape), _resident(w_in.shape), _resident(conv_w.shape), _resident(w_out.shape)],
        out_specs=pl.BlockSpec((tm, D), lambda b, i: (b * nt + i, 0)),
        out_shape=jax.ShapeDtypeStruct((T, D), F32),
        scratch_shapes=[pltpu.VMEM((8 + tm, D), F32)],
        compiler_params=_cparams(("parallel", "arbitrary")),
        name="short_conv",
    )(h2, nw, w_in.astype(BF16), conv_w, w_out.astype(BF16))


_NQ = NSA_HEADS * NSA_DH
_KVW = NSA_KV_HEADS * NSA_DH
_OFF_KC, _OFF_VC, _OFF_KS, _OFF_VS, _OFF_KW, _OFF_VW = (_NQ + i * _KVW for i in range(6))
_OFF_G = _NQ + 6 * _KVW


def _half(slab, j):
    if j:
        slab = pltpu.roll(slab, NSA_DH, axis=1)
    return slab[:, 0:NSA_DH]


def _nsa_prep_kernel(p_ref, pos_ref, inv_ref, qw_ref, ksw_ref, kww_ref, bd_ref,
                     qn_ref, qr_ref, ks_ref, kw_ref, vs_ref, vw_ref, kc_ref, vc_ref, g_ref):
    tm = p_ref.shape[0]
    dh = NSA_DH
    ang = pos_ref[...].astype(F32) * inv_ref[...]
    cos = jnp.cos(ang)
    sin = jnp.sin(ang)
    lane = lax.broadcasted_iota(jnp.int32, (tm, LANES), 1)
    first_half = (lane % dh) < (dh // 2)
    bd = bd_ref[...]

    def seg_rms(x, w):
        ss = _dot2_lhs(x * x, bd)
        return x * lax.rsqrt(ss * (1.0 / dh) + EPS) * w

    def rope(x):
        rot = jnp.where(first_half, -pltpu.roll(x, LANES - dh // 2, axis=1), pltpu.roll(x, dh // 2, axis=1))
        return x * cos + rot * sin

    def put(ref, slab, c):
        for j in range(2):
            ref[0, 2 * c + j] = _half(slab, j).astype(ref.dtype)

    scale = dh ** -0.5
    for c in range(_NQ // LANES):
        xn = seg_rms(p_ref[:, c * LANES:(c + 1) * LANES], qw_ref[...])
        put(qn_ref, xn * scale, c)
        put(qr_ref, rope(xn) * scale, c)
    for c in range(_KVW // LANES):
        put(ks_ref, rope(seg_rms(p_ref[:, _OFF_KS + c * LANES:_OFF_KS + (c + 1) * LANES], ksw_ref[...])), c)
        put(kw_ref, rope(seg_rms(p_ref[:, _OFF_KW + c * LANES:_OFF_KW + (c + 1) * LANES], kww_ref[...])), c)
        put(vs_ref, p_ref[:, _OFF_VS + c * LANES:_OFF_VS + (c + 1) * LANES], c)
        put(vw_ref, p_ref[:, _OFF_VW + c * LANES:_OFF_VW + (c + 1) * LANES], c)
        put(kc_ref, p_ref[:, _OFF_KC + c * LANES:_OFF_KC + (c + 1) * LANES], c)
        put(vc_ref, p_ref[:, _OFF_VC + c * LANES:_OFF_VC + (c + 1) * LANES], c)
    g_ref[...] = jax.nn.sigmoid(p_ref[:, _OFF_G:_OFF_G + LANES])


def _nsa_prep(proj, pos, inv, qw, ksw, kww, bd, B, S, tm=256):
    H, Hk, dh = NSA_HEADS, NSA_KV_HEADS, NSA_DH
    nt = S // tm
    NP = proj.shape[1]

    def hm(nh, dt):
        return (pl.BlockSpec((1, nh, tm, dh), lambda b, i: (b, 0, i, 0)),
                jax.ShapeDtypeStruct((B, nh, S, dh), dt))

    outs = [hm(H, BF16), hm(H, BF16), hm(Hk, BF16), hm(Hk, BF16), hm(Hk, BF16), hm(Hk, BF16),
            hm(Hk, F32), hm(Hk, F32),
            (pl.BlockSpec((tm, LANES), lambda b, i: (b * nt + i, 0)), jax.ShapeDtypeStruct((B * S, LANES), F32))]
    return pl.pallas_call(
        _nsa_prep_kernel,
        grid=(B, nt),
        in_specs=[pl.BlockSpec((tm, NP), lambda b, i: (b * nt + i, 0)),
                  pl.BlockSpec((tm, 1), lambda b, i: (b * nt + i, 0)),
                  _resident(inv.shape), _resident(qw.shape), _resident(ksw.shape), _resident(kww.shape),
                  _resident(bd.shape)],
        out_specs=[o[0] for o in outs],
        out_shape=[o[1] for o in outs],
        compiler_params=_cparams(("parallel", "parallel")),
        name="nsa_prep",
    )(proj, pos, inv, qw, ksw, kww, bd)


def _gelu_tanh(x):
    return 0.5 * x * (1.0 + jnp.tanh(np.sqrt(2.0 / np.pi).astype(np.float32) * (x + 0.044715 * (x * x * x))))


def _cmp_kernel(kc_ref, vc_ref, pe_ref, w1_ref, b1_ref, w2_ref, b2_ref, knw_ref, kco_ref, vco_ref):
    nch = kc_ref.shape[2]
    for i, (src, dst) in enumerate(((kc_ref, kco_ref), (vc_ref, vco_ref))):
        x = src[0, 0]
        p1 = _dot((x + pe_ref[i, 0:1]).astype(BF16), w1_ref[i, 0])
        p2 = _dot((x + pe_ref[i, 1:2]).astype(BF16), w1_ref[i, 1])
        hid = _gelu_tanh(p1 + pltpu.roll(p2, nch - 1, axis=0) + b1_ref[i])
        out = _dot(hid.astype(BF16), w2_ref[i]) + b2_ref[i]
        if i == 0:
            out = _rms(out, knw_ref[...])
        dst[0, 0] = out.astype(BF16)


def _nsa_compress(kc, vc, pe, w1, b1, w2, b2, knw):
    B, Hk, nch, cw = kc.shape
    dh = NSA_DH
    spec = pl.BlockSpec((1, 1, nch, cw), lambda b, h: (b, h, 0, 0))
    ospec = pl.BlockSpec((1, 1, nch, dh), lambda b, h: (b, h, 0, 0))
    oshape = jax.ShapeDtypeStruct((B, Hk, nch, dh), BF16)
    return pl.pallas_call(
        _cmp_kernel,
        grid=(B, Hk),
        in_specs=[spec, spec, _resident(pe.shape), _resident(w1.shape), _resident(b1.shape),
                  _resident(w2.shape), _resident(b2.shape), _resident(knw.shape)],
        out_specs=[ospec, ospec],
        out_shape=[oshape, oshape],
        compiler_params=_cparams(("parallel", "parallel")),
        name="nsa_compress",
    )(kc, vc, pe, w1, b1, w2, b2, knw)


def _cmpsel_kernel(q_ref, kc_ref, vc_ref, ovl_ref, oc_ref, sel_ref, *, nsel):
    G, tq, dh = q_ref.shape[2:]
    nck = kc_ref.shape[2]
    ns = ovl_ref.shape[1]
    s0 = pl.program_id(2) * tq
    q = q_ref[0, 0].reshape(G * tq, dh)
    sc = _dot_nt(q, kc_ref[0, 0]).reshape(G, tq, nck)
    tpos3 = s0 + lax.broadcasted_iota(jnp.int32, (1, tq, 1), 1)
    cend = lax.broadcasted_iota(jnp.int32, (1, 1, nck), 2) * CMP_STRIDE + (CMP_BLOCK - 1)
    cmask = cend <= tpos3
    scm = jnp.where(cmask, sc, NEG)
    e = jnp.exp(scm - jnp.max(scm, axis=-1, keepdims=True))
    p = e / jnp.sum(e, axis=-1, keepdims=True)
    p = jnp.where(tpos3 >= CMP_BLOCK - 1, p, 0.0)
    oc_ref[0, 0] = _dot(p.reshape(G * tq, nck).astype(BF16), vc_ref[0, 0]).reshape(G, tq, dh)

    imp = _dot2_lhs(jnp.sum(p, axis=0), ovl_ref[...])
    tpos = s0 + lax.broadcasted_iota(jnp.int32, (tq, 1), 0)
    blk = lax.broadcasted_iota(jnp.int32, (tq, ns), 1)
    svalid = blk * SLC_BLOCK <= tpos
    dist = tpos // SLC_BLOCK - blk
    forced = (blk == 0) | ((dist >= 0) & (dist < N_LOCAL))
    score = jnp.where(svalid & forced, FORCE, jnp.where(svalid, imp, -1.0))
    rank = jnp.zeros((tq, ns), F32)
    for j in range(ns):
        col = score[:, j:j + 1]
        ahead = (col > score) | ((col == score) & (blk > j))
        rank = rank + jnp.where(ahead, 1.0, 0.0)
    sel_ref[0, 0] = jnp.where((rank < nsel) & (score >= 0.0), 1.0, 0.0).astype(BF16)


def _nsa_cmpsel(qn, kcc, vcc, ovl, S, tq=128):
    B, Hk, G, dh = qn.shape[0], NSA_KV_HEADS, NSA_GROUP, NSA_DH
    q5 = qn.reshape(B, Hk, G, S, dh)
    nck = kcc.shape[2]
    ns = ovl.shape[1]
    nsel = min(SLC_TOPK, ns)
    return pl.pallas_call(
        functools.partial(_cmpsel_kernel, nsel=nsel),
        grid=(B, Hk, S // tq),
        in_specs=[pl.BlockSpec((1, 1, G, tq, dh), lambda b, h, i: (b, h, 0, i, 0)),
                  pl.BlockSpec((1, 1, nck, dh), lambda b, h, i: (b, h, 0, 0)),
                  pl.BlockSpec((1, 1, nck, dh), lambda b, h, i: (b, h, 0, 0)),
                  _resident(ovl.shape)],
        out_specs=[pl.BlockSpec((1, 1, G, tq, dh), lambda b, h, i: (b, h, 0, i, 0)),
                   pl.BlockSpec((1, 1, tq, ns), lambda b, h, i: (b, h, i, 0))],
        out_shape=[jax.ShapeDtypeStruct((B, Hk, G, S, dh), F32),
                   jax.ShapeDtypeStruct((B, Hk, S, ns), BF16)],
        compiler_params=_cparams(("parallel", "parallel", "parallel")),
        name="nsa_cmpsel",
    )(q5, kcc, vcc, ovl)


def _sel_kernel(q_ref, k_ref, v_ref, sel_ref, o_ref):
    G, tq, dh = q_ref.shape[2:]
    ns = sel_ref.shape[3]
    tk = tq
    i = pl.program_id(2)
    s0 = i * tq
    q = q_ref[0, 0].reshape(G * tq, dh)
    sel = sel_ref[0, 0]
    tpos = s0 + lax.broadcasted_iota(jnp.int32, (tq, 1), 0)
    eb = lax.broadcasted_iota(jnp.int32, (ns, tk), 0)
    ek = lax.broadcasted_iota(jnp.int32, (ns, tk), 1)
    kidx = lax.broadcasted_iota(jnp.int32, (1, tk), 1)

    def body(kt, carry):
        m, l, acc = carry
        k0 = pl.multiple_of(kt * tk, tk)
        k = k_ref[0, 0, pl.ds(k0, tk), :]
        v = v_ref[0, 0, pl.ds(k0, tk), :]
        s = _dot_nt(q, k).reshape(G, tq, tk)
        expand = jnp.where((k0 + ek) // SLC_BLOCK == eb, 1.0, 0.0).astype(BF16)
        selx = _dot(sel, expand)
        msk = (selx > 0.5) & (k0 + kidx <= tpos)
        s = jnp.where(msk[None], s, NEG)
        m_new = jnp.maximum(m, jnp.max(s, axis=-1, keepdims=True))
        alpha = jnp.exp(m - m_new)
        p = jnp.exp(s - m_new)
        l = alpha * l + jnp.sum(p, axis=-1, keepdims=True)
        acc = alpha * acc + _dot(p.reshape(G * tq, tk).astype(BF16), v).reshape(G, tq, dh)
        return m_new, l, acc

    m0 = jnp.full((G, tq, 1), NEG, F32)
    l0 = jnp.zeros((G, tq, 1), F32)
    a0 = jnp.zeros((G, tq, dh), F32)
    m, l, acc = lax.fori_loop(0, i + 1, body, (m0, l0, a0))
    o_ref[0, 0] = acc / l


def _nsa_select(qr, ks, vs, sel, S, tq=128):
    B, Hk, G, dh = qr.shape[0], NSA_KV_HEADS, NSA_GROUP, NSA_DH
    q5 = qr.reshape(B, Hk, G, S, dh)
    ns = sel.shape[3]
    return pl.pallas_call(
        _sel_kernel,
        grid=(B, Hk, S // tq),
        in_specs=[pl.BlockSpec((1, 1, G, tq, dh), lambda b, h, i: (b, h, 0, i, 0)),
                  pl.BlockSpec((1, 1, S, dh), lambda b, h, i: (b, h, 0, 0)),
                  pl.BlockSpec((1, 1, S, dh), lambda b, h, i: (b, h, 0, 0)),
                  pl.BlockSpec((1, 1, tq, ns), lambda b, h, i: (b, h, i, 0))],
        out_specs=pl.BlockSpec((1, 1, G, tq, dh), lambda b, h, i: (b, h, 0, i, 0)),
        out_shape=jax.ShapeDtypeStruct((B, Hk, G, S, dh), F32),
        compiler_params=_cparams(("parallel", "parallel", "arbitrary")),
        name="nsa_select",
    )(q5, ks, vs, sel)


def _win_kernel(q_ref, k_ref, v_ref, o_ref):
    G, tq, dh = q_ref.shape[2:]
    s0 = pl.program_id(2) * tq
    q = q_ref[0, 0].reshape(G * tq, dh)
    tpos = s0 + lax.broadcasted_iota(jnp.int32, (tq, 1), 0)
    kidx = lax.broadcasted_iota(jnp.int32, (1, tq), 1)
    nt = WINDOW // tq + 1
    scores, starts = [], []
    for j in range(nt):
        start = s0 - WINDOW + j * tq
        st = pl.multiple_of(jnp.maximum(start, 0), tq)
        kpos = jnp.where(start >= 0, st, S_FAR) + kidx
        msk = (kpos <= tpos) & (kpos > tpos - WINDOW)
        s = _dot_nt(q, k_ref[0, 0, pl.ds(st, tq), :]).reshape(G, tq, tq)
        scores.append(jnp.where(msk[None], s, NEG))
        starts.append(st)
    s_all = jnp.concatenate(scores, axis=-1)
    e = jnp.exp(s_all - jnp.max(s_all, axis=-1, keepdims=True))
    p = e / jnp.sum(e, axis=-1, keepdims=True)
    acc = jnp.zeros((G * tq, dh), F32)
    for j in range(nt):
        pj = p[:, :, j * tq:(j + 1) * tq].reshape(G * tq, tq).astype(BF16)
        acc = acc + _dot(pj, v_ref[0, 0, pl.ds(starts[j], tq), :])
    o_ref[0, 0] = acc.reshape(G, tq, dh)


def _nsa_window(qr, kw, vw, S, tq=128):
    B, Hk, G, dh = qr.shape[0], NSA_KV_HEADS, NSA_GROUP, NSA_DH
    q5 = qr.reshape(B, Hk, G, S, dh)
    return pl.pallas_call(
        _win_kernel,
        grid=(B, Hk, S // tq),
        in_specs=[pl.BlockSpec((1, 1, G, tq, dh), lambda b, h, i: (b, h, 0, i, 0)),
                  pl.BlockSpec((1, 1, S, dh), lambda b, h, i: (b, h, 0, 0)),
                  pl.BlockSpec((1, 1, S, dh), lambda b, h, i: (b, h, 0, 0))],
        out_specs=pl.BlockSpec((1, 1, G, tq, dh), lambda b, h, i: (b, h, 0, i, 0)),
        out_shape=jax.ShapeDtypeStruct((B, Hk, G, S, dh), F32),
        compiler_params=_cparams(("parallel", "parallel", "arbitrary")),
        name="nsa_window",
    )(q5, kw, vw)


def _nsa_out_kernel(oc_ref, os_ref, ow_ref, g_ref, e_ref, w_ref, h_ref, o_ref):
    g = g_ref[...]
    acc = None
    for i, r in enumerate((oc_ref, os_ref, ow_ref)):
        term = _dot2_lhs(g, e_ref[i]) * r[...]
        acc = term if acc is None else acc + term
    o_ref[...] = h_ref[...] + _dot(acc.astype(BF16), w_ref[...])


def _nsa_out(oc, osel, ow, gates, expand, w_out, h2, tm=256):
    T, D = h2.shape
    N = oc.shape[1]
    row = lambda n: pl.BlockSpec((tm, n), lambda i: (i, 0))
    return pl.pallas_call(
        _nsa_out_kernel,
        grid=(T // tm,),
        in_specs=[row(N), row(N), row(N), row(LANES), _resident(expand.shape), _resident(w_out.shape), row(D)],
        out_specs=row(D),
        out_shape=jax.ShapeDtypeStruct((T, D), F32),
        compiler_params=_cparams(("parallel",)),
        name="nsa_out",
    )(oc, osel, ow, gates, expand, w_out, h2)


def _nsa_constants(S):
    H, dh = NSA_HEADS, NSA_DH
    half = dh // 2
    inv = 1.0 / (ROPE_THETA ** (np.arange(0, dh, 2, dtype=np.float32) / dh))
    inv = np.tile(inv.astype(np.float32), LANES // half).reshape(1, LANES)
    lane = np.arange(LANES)
    bd = (lane[:, None] // dh == lane[None, :] // dh).astype(np.float32)
    nck = S // CMP_STRIDE
    ns = S // SLC_BLOCK
    ci = np.arange(nck)[:, None]
    sj = np.arange(ns)[None, :]
    ovl = np.clip(np.minimum(ci * CMP_STRIDE + CMP_BLOCK, (sj + 1) * SLC_BLOCK)
                  - np.maximum(ci * CMP_STRIDE, sj * SLC_BLOCK), 0, None).astype(np.float32) / CMP_BLOCK
    expand = np.zeros((3, LANES, H * dh), np.float32)
    for i in range(3):
        for hh in range(H):
            expand[i, 3 * hh + i, hh * dh:(hh + 1) * dh] = 1.0
    return (jnp.asarray(inv), jnp.asarray(bd, BF16), jnp.asarray(ovl, BF16), jnp.asarray(expand, BF16))


def _nsa_mixer(h2, nw, pos, w_in, q_norm, k_norm, cmp_pe, cmp_w1, cmp_b1, cmp_w2, cmp_b2, w_out, B, S):
    H, Hk, dh = NSA_HEADS, NSA_KV_HEADS, NSA_DH
    inv, bd, ovl, expand = _nsa_constants(S)
    n_used = w_in.shape[1]
    w_in_p = jnp.pad(w_in, ((0, 0), (0, -n_used % LANES))).astype(BF16)
    proj = _norm_matmul(h2, nw, w_in_p, tn=w_in_p.shape[1] // 3)
    tile2 = lambda w: jnp.tile(w, LANES // dh).reshape(1, LANES)
    qn, qr, ks, kw, vs, vw, kc, vc, gates = _nsa_prep(
        proj, pos.reshape(B * S, 1), inv, tile2(q_norm), tile2(k_norm[1]), tile2(k_norm[2]), bd, B, S)

    nch = S // CMP_STRIDE
    cw = CMP_STRIDE * dh
    kcc, vcc = _nsa_compress(
        kc.reshape(B, Hk, nch, cw), vc.reshape(B, Hk, nch, cw),
        cmp_pe.reshape(2, 2, cw), cmp_w1.reshape(2, 2, cw, CMP_HIDDEN).astype(BF16),
        cmp_b1.reshape(2, 1, CMP_HIDDEN), cmp_w2.astype(BF16), cmp_b2.reshape(2, 1, dh),
        k_norm[0].reshape(1, dh))

    oc, sel = _nsa_cmpsel(qn, kcc, vcc, ovl, S)
    osel = _nsa_select(qr, ks, vs, sel, S)
    ow = _nsa_window(qr, kw, vw, S)
    tok = lambda o: o.transpose(0, 3, 1, 2, 4).reshape(B * S, H * dh)
    return _nsa_out(tok(oc), tok(osel), tok(ow), gates, expand, w_out.astype(BF16), h2)


def kernel(x, positions, ffn_norm, ffn_w_gate_up, ffn_w_down, mixer_norm, gdn_w_in, gdn_conv_w, gdn_A_log, gdn_dt_bias, gdn_out_norm, gdn_w_out, sc_w_in, sc_conv_w, sc_w_out, nsa_w_in, nsa_q_norm, nsa_k_norm, nsa_cmp_pe, nsa_cmp_w1, nsa_cmp_b1, nsa_cmp_w2, nsa_cmp_b2, nsa_w_out):
    B, S, D = x.shape
    depth = ffn_norm.shape[0]
    h = x.reshape(B * S, D)
    for layer in range(depth):
        h = _ffn(h, ffn_norm[layer, 0].reshape(1, D), *_prep_ffn_weights(ffn_w_gate_up[layer, 0], ffn_w_down[layer, 0]))
        nw = mixer_norm[layer].reshape(1, D)
        kind = layer % N_MIXERS
        j = layer // N_MIXERS
        if kind == 0:
            h = _gdn_mixer(h, nw, gdn_w_in[j], gdn_conv_w[j], gdn_A_log[j], gdn_dt_bias[j],
                           gdn_out_norm[j], gdn_w_out[j], B, S)
        elif kind == 1:
            h = _sc_mixer(h, nw, sc_w_in[j], sc_conv_w[j], sc_w_out[j], B, S)
        else:
            h = _nsa_mixer(h, nw, positions, nsa_w_in[j], nsa_q_norm[j], nsa_k_norm[j], nsa_cmp_pe[j],
                           nsa_cmp_w1[j], nsa_cmp_b1[j], nsa_cmp_w2[j], nsa_cmp_b2[j], nsa_w_out[j], B, S)
        h = _ffn(h, ffn_norm[layer, 1].reshape(1, D), *_prep_ffn_weights(ffn_w_gate_up[layer, 1], ffn_w_down[layer, 1]))
    return h.reshape(B, S, D)
```

```python
import functools

import numpy as np
import jax
import jax.numpy as jnp
from jax import lax
from jax.experimental import pallas as pl
from jax.experimental.pallas import tpu as pltpu

F32 = jnp.float32
BF16 = jnp.bfloat16

EPS = 1e-6
NEG = -1e30
FORCE = 1e9
S_FAR = 1 << 30
LANES = 128

GDN_HEADS = 8
GDN_DK = 128
GDN_DV = 128
GDN_CONV = 4
GDN_CHUNK = 64
SC_WIDTH = 3
NSA_HEADS = 16
NSA_KV_HEADS = 4
NSA_GROUP = NSA_HEADS // NSA_KV_HEADS
NSA_DH = 64
CMP_BLOCK = 32
CMP_STRIDE = 16
CMP_HIDDEN = 256
SLC_BLOCK = 64
SLC_TOPK = 16
N_LOCAL = 2
WINDOW = 512
ROPE_THETA = 10000.0
N_MIXERS = 3
GATE_ROWS = 16

VMEM_LIMIT = 56 * 1024 * 1024


def _cparams(sem):
    return pltpu.CompilerParams(dimension_semantics=sem, vmem_limit_bytes=VMEM_LIMIT)


def _resident(shape):
    nd = len(shape)
    return pl.BlockSpec(shape, lambda *_: (0,) * nd, pipeline_mode=pl.Buffered(1))


def _rms(x, w):
    return x * lax.rsqrt(jnp.mean(x * x, axis=-1, keepdims=True) + EPS) * w


def _silu(x):
    return x * jax.nn.sigmoid(x)


def _softplus(x):
    return jnp.maximum(x, 0.0) + jnp.log1p(jnp.exp(-jnp.abs(x)))


def _bf(x):
    return x.astype(BF16)


def _dot(a, b):
    return jnp.dot(a, b, preferred_element_type=F32)


def _dot_nt(a, b):
    return lax.dot_general(a, b, (((1,), (1,)), ((), ())), preferred_element_type=F32)


def _split(x):
    hi = x.astype(BF16)
    lo = (x - hi.astype(F32)).astype(BF16)
    return hi, lo


def _dot2_lhs(a, b_bf16):
    ah, al = _split(a)
    return _dot(ah, b_bf16) + _dot(al, b_bf16)


def _ffn_kernel(x_ref, nw_ref, wg_ref, wu_ref, wd_ref, o_ref, acc_ref, *, nc):
    x = x_ref[...]
    xn = _rms(x, nw_ref[...]).astype(BF16)
    acc_ref[...] = jnp.zeros_like(acc_ref)

    def body(c, carry):
        g = _dot(xn, wg_ref[c])
        u = _dot(xn, wu_ref[c])
        a = (_silu(g) * u).astype(BF16)
        acc_ref[...] += _dot(a, wd_ref[c])
        return carry

    lax.fori_loop(0, nc, body, 0)
    o_ref[...] = x + 0.5 * acc_ref[...]


def _ffn(h2, nw, wg, wu, wd, tm=512):
    T, D = h2.shape
    nc = wg.shape[0]
    return pl.pallas_call(
        functools.partial(_ffn_kernel, nc=nc),
        grid=(T // tm,),
        in_specs=[pl.BlockSpec((tm, D), lambda i: (i, 0)),
                  _resident(nw.shape), _resident(wg.shape), _resident(wu.shape), _resident(wd.shape)],
        out_specs=pl.BlockSpec((tm, D), lambda i: (i, 0)),
        out_shape=jax.ShapeDtypeStruct((T, D), F32),
        scratch_shapes=[pltpu.VMEM((tm, D), F32)],
        compiler_params=_cparams(("parallel",)),
        name="ffn",
    )(h2, nw, wg, wu, wd)


def _prep_ffn_weights(w_gate_up, w_down, tf=256):
    D, F2 = w_gate_up.shape
    Fh = F2 // 2
    nc = Fh // tf
    wg = w_gate_up[:, :Fh].reshape(D, nc, tf).transpose(1, 0, 2).astype(BF16)
    wu = w_gate_up[:, Fh:].reshape(D, nc, tf).transpose(1, 0, 2).astype(BF16)
    wd = w_down.reshape(nc, tf, D).astype(BF16)
    return wg, wu, wd


def _norm_matmul_kernel(x_ref, nw_ref, w_ref, o_ref, *, tn):
    xn = _rms(x_ref[...], nw_ref[...]).astype(BF16)
    for c in range(w_ref.shape[1] // tn):
        o_ref[:, c * tn:(c + 1) * tn] = _dot(xn, w_ref[:, c * tn:(c + 1) * tn])


def _norm_matmul(h2, nw, w, tn, tm=256):
    T, D = h2.shape
    N = w.shape[1]
    return pl.pallas_call(
        functools.partial(_norm_matmul_kernel, tn=tn),
        grid=(T // tm,),
        in_specs=[pl.BlockSpec((tm, D), lambda i: (i, 0)), _resident(nw.shape), _resident(w.shape)],
        out_specs=pl.BlockSpec((tm, N), lambda i: (i, 0)),
        out_shape=jax.ShapeDtypeStruct((T, N), F32),
        compiler_params=_cparams(("parallel",)),
        name="norm_matmul",
    )(h2, nw, w)


def _matmul_res_kernel(a_ref, w_ref, h_ref, o_ref):
    o_ref[...] = h_ref[...] + _dot(a_ref[...], w_ref[...])


def _matmul_res(a, w, h2, tm=512):
    T, D = h2.shape
    K = a.shape[1]
    return pl.pallas_call(
        _matmul_res_kernel,
        grid=(T // tm,),
        in_specs=[pl.BlockSpec((tm, K), lambda i: (i, 0)), _resident(w.shape),
                  pl.BlockSpec((tm, D), lambda i: (i, 0))],
        out_specs=pl.BlockSpec((tm, D), lambda i: (i, 0)),
        out_shape=jax.ShapeDtypeStruct((T, D), F32),
        compiler_params=_cparams(("parallel",)),
        name="matmul_res",
    )(a, w, h2)


def _gdn_kernel(qkv_ref, gate_ref, ab_ref, cw_ref, alog_ref, dtb_ref, onw_ref, o_ref, xbuf_ref, s_ref, *, nch):
    C = GDN_CHUNK
    H, DK, DV = GDN_HEADS, GDN_DK, GDN_DV
    R = nch * C
    n = pl.program_id(1)

    @pl.when(n == 0)
    def _():
        xbuf_ref[0:8, :] = jnp.zeros((8, xbuf_ref.shape[1]), F32)
        s_ref[...] = jnp.zeros_like(s_ref)

    raw = qkv_ref[...]
    xbuf_ref[8:8 + R, :] = raw
    cw = cw_ref[...]
    y = (xbuf_ref[5:5 + R, :] * cw[0:1] + xbuf_ref[6:6 + R, :] * cw[1:2]
         + xbuf_ref[7:7 + R, :] * cw[2:3] + raw * cw[3:4])
    xbuf_ref[0:8, :] = raw[R - 8:R]
    y = _silu(y)

    ab = ab_ref[...]
    g_all = -jnp.exp(alog_ref[...]) * _softplus(ab + dtb_ref[...])
    beta_all = jax.nn.sigmoid(ab)
    rowc = lax.broadcasted_iota(jnp.int32, (R, LANES), 0) % C
    gc = g_all
    s = 1
    while s < C:
        gc = gc + jnp.where(rowc >= s, pltpu.roll(gc, s, axis=0), 0.0)
        s *= 2
    egc = jnp.exp(gc)
    gcc = [gc[c * C:(c + 1) * C] for c in range(nch)]
    gct = [g.T for g in gcc]
    glast = [g[C - 1:C, :] for g in gcc]
    ekd = [jnp.exp(glast[c] - gcc[c]) for c in range(nch)]
    egl = [jnp.exp(glast[c]) for c in range(nch)]

    ri = lax.broadcasted_iota(jnp.int32, (C, C), 0)
    ci = lax.broadcasted_iota(jnp.int32, (C, C), 1)
    lower = ri >= ci
    strict = ri > ci

    pairs = [(c, h) for c in range(nch) for h in range(H)]
    dec, qnb, knb, kbb, sol, qd, kdt = {}, {}, {}, {}, {}, {}, {}
    for p in pairs:
        c, h = p
        rows = slice(c * C, (c + 1) * C)
        gcol = gcc[c][:, h:h + 1]
        grow = gct[c][h:h + 1, :]
        beta = beta_all[rows, H + h:H + h + 1]
        e = egc[rows, h:h + 1]
        dec[p] = jnp.where(lower, jnp.exp(jnp.where(lower, gcol - grow, 0.0)), 0.0)
        q = y[rows, h * DK:(h + 1) * DK]
        k = y[rows, H * DK + h * DK:H * DK + (h + 1) * DK]
        v = y[rows, 2 * H * DK + h * DV:2 * H * DK + (h + 1) * DV]
        qn = q * lax.rsqrt(jnp.sum(q * q, -1, keepdims=True) + EPS) * (DK ** -0.5)
        kn = k * lax.rsqrt(jnp.sum(k * k, -1, keepdims=True) + EPS)
        kb = kn * beta
        qnb[p], knb[p], kbb[p] = _bf(qn), _bf(kn), _bf(kb)
        sol[p] = jnp.concatenate([v * beta, kb * e], axis=1)
        qd[p] = qn * e
        kdt[p] = _bf((kn * ekd[c][:, h:h + 1]).T)

    kk = {p: _dot_nt(kbb[p], knb[p]) for p in pairs}
    qk = {p: _dot_nt(qnb[p], knb[p]) for p in pairs}
    nil = {p: jnp.where(strict, -kk[p] * dec[p], 0.0) for p in pairs}
    attn = {p: _bf(jnp.where(lower, qk[p] * dec[p], 0.0)) for p in pairs}
    x = {p: _bf(nil[p]) for p in pairs}
    pw = 2
    while pw < C:
        x2 = {p: _dot(x[p], x[p]) for p in pairs}
        x = {p: _bf(x2[p]) for p in pairs}
        nil = {p: nil[p] + x2[p] + _dot(_bf(nil[p]), x[p]) for p in pairs}
        pw *= 2
    sol = {p: sol[p] + _dot(_bf(nil[p]), _bf(sol[p])) for p in pairs}

    st = [s_ref[h] for h in range(H)]
    for c in range(nch):
        r = [_dot(_bf(jnp.concatenate([sol[(c, h)][:, DV:], qd[(c, h)]], axis=0)), _bf(st[h])) for h in range(H)]
        vb = [_bf(sol[(c, h)][:, :DV] - r[h][:C]) for h in range(H)]
        o = [r[h][C:] + _dot(attn[(c, h)], vb[h]) for h in range(H)]
        st = [st[h] * egl[c][:, h:h + 1] + _dot(kdt[(c, h)], vb[h]) for h in range(H)]
        for h in range(H):
            gt = gate_ref[c * C:(c + 1) * C, h * DV:(h + 1) * DV]
            o_ref[c * C:(c + 1) * C, h * DV:(h + 1) * DV] = _bf(_rms(o[h], onw_ref[...]) * _silu(gt))
    for h in range(H):
        s_ref[h] = st[h]


def _gdn_core(proj, cw, alog, dtb, onw, B, S, nch=2):
    R = nch * GDN_CHUNK
    NQ = 2 * GDN_HEADS * GDN_DK + GDN_HEADS * GDN_DV
    NG = GDN_HEADS * GDN_DV
    nstep = S // R
    T = B * S
    return pl.pallas_call(
        functools.partial(_gdn_kernel, nch=nch),
        grid=(B, nstep),
        in_specs=[pl.BlockSpec((R, NQ), lambda b, n: (b * nstep + n, 0)),
                  pl.BlockSpec((R, NG), lambda b, n: (b * nstep + n, NQ // NG)),
                  pl.BlockSpec((R, LANES), lambda b, n: (b * nstep + n, (NQ + NG) // LANES)),
                  _resident(cw.shape), _resident(alog.shape), _resident(dtb.shape), _resident(onw.shape)],
        out_specs=pl.BlockSpec((R, NG), lambda b, n: (b * nstep + n, 0)),
        out_shape=jax.ShapeDtypeStruct((T, NG), BF16),
        scratch_shapes=[pltpu.VMEM((8 + R, NQ), F32), pltpu.VMEM((GDN_HEADS, GDN_DK, GDN_DV), F32)],
        compiler_params=_cparams(("parallel", "arbitrary")),
        name="gdn_core",
    )(proj, proj, proj, cw, alog, dtb, onw)


def _gdn_mixer(h2, nw, w_in, conv_w, a_log, dt_bias, out_norm, w_out, B, S):
    H = GDN_HEADS
    n_used = w_in.shape[1]
    n_pad = -n_used % LANES
    w_in_p = jnp.pad(w_in, ((0, 0), (0, n_pad))).astype(BF16)
    proj = _norm_matmul(h2, nw, w_in_p, tn=w_in_p.shape[1] // 3)
    lane_pad = LANES - H
    alog = jnp.pad(a_log, (0, lane_pad)).reshape(1, LANES)
    dtb = jnp.pad(dt_bias, (0, lane_pad)).reshape(1, LANES)
    o = _gdn_core(proj, conv_w, alog, dtb, out_norm.reshape(1, GDN_DV), B, S)
    return _matmul_res(o, w_out.astype(BF16), h2)


def _sc_kernel(h_ref, nw_ref, win_ref, cw_ref, wout_ref, o_ref, cx_ref):
    tm, D = h_ref.shape
    i = pl.program_id(1)

    @pl.when(i == 0)
    def _():
        cx_ref[0:8, :] = jnp.zeros((8, D), F32)

    x = h_ref[...]
    xn = _rms(x, nw_ref[...]).astype(BF16)
    bg = _dot(xn, win_ref[:, 0:D])
    cx = _dot(xn, win_ref[:, D:2 * D]) * _dot(xn, win_ref[:, 2 * D:3 * D])
    cx_ref[8:8 + tm, :] = cx
    cw = cw_ref[...]
    y = cx_ref[6:6 + tm, :] * cw[0:1] + cx_ref[7:7 + tm, :] * cw[1:2] + cx * cw[2:3]
    cx_ref[0:8, :] = cx[tm - 8:tm]
    o_ref[...] = x + _dot((bg * y).astype(BF16), wout_ref[...])


def _sc_mixer(h2, nw, w_in, conv_w, w_out, B, S, tm=256):
    T, D = h2.shape
    nt = S // tm
    return pl.pallas_call(
        _sc_kernel,
        grid=(B, nt),
        in_specs=[pl.BlockSpec((tm, D), lambda b, i: (b * nt + i, 0)),
                  _resident(nw.shape), _resident(w_in.shape), _resident(conv_w.shape), _resident(w_out.shape)],
        out_specs=pl.BlockSpec((tm, D), lambda b, i: (b * nt + i, 0)),
        out_shape=jax.ShapeDtypeStruct((T, D), F32),
        scratch_shapes=[pltpu.VMEM((8 + tm, D), F32)],
        compiler_params=_cparams(("parallel", "arbitrary")),
        name="short_conv",
    )(h2, nw, w_in.astype(BF16), conv_w, w_out.astype(BF16))


_NQ = NSA_HEADS * NSA_DH
_KVW = NSA_KV_HEADS * NSA_DH
_OFF_KC, _OFF_VC, _OFF_KS, _OFF_VS, _OFF_KW, _OFF_VW = (_NQ + i * _KVW for i in range(6))
_OFF_G = _NQ + 6 * _KVW
KT = 128
SEL_TK = 512
LOG2E = float(np.log2(np.e))


def _half(slab, j):
    if j:
        slab = pltpu.roll(slab, NSA_DH, axis=1)
    return slab[:, 0:NSA_DH]


def _nsa_prep_kernel(p_ref, pos_ref, inv_ref, qw_ref, ksw_ref, kww_ref, bd_ref,
                     qn_ref, qr_ref, ks_ref, kw_ref, vs_ref, vw_ref, kc_ref, vc_ref, g_ref):
    tm = p_ref.shape[0]
    dh = NSA_DH
    ang = pos_ref[...].astype(F32) * inv_ref[...]
    cos = jnp.cos(ang)
    sin = jnp.sin(ang)
    lane = lax.broadcasted_iota(jnp.int32, (tm, LANES), 1)
    first_half = (lane % dh) < (dh // 2)
    bd = bd_ref[...]

    def seg_rms(x, w):
        ss = _dot2_lhs(x * x, bd)
        return x * lax.rsqrt(ss * (1.0 / dh) + EPS) * w

    def rope(x):
        rot = jnp.where(first_half, -pltpu.roll(x, LANES - dh // 2, axis=1), pltpu.roll(x, dh // 2, axis=1))
        return x * cos + rot * sin

    def put(ref, slab, c):
        for j in range(2):
            ref[0, 2 * c + j] = _half(slab, j).astype(ref.dtype)

    def put_t(ref, slab, c):
        t = slab.T
        for j in range(2):
            ref[0, 2 * c + j] = t[j * dh:(j + 1) * dh].astype(ref.dtype)

    def put_t_tiles(ref, slab, c):
        t = slab.T
        for j in range(2):
            for tt in range(tm // KT):
                ref[0, 2 * c + j, tt] = t[j * dh:(j + 1) * dh, tt * KT:(tt + 1) * KT].astype(ref.dtype)

    def put_aug(ref, slab, c):
        row = lax.broadcasted_iota(jnp.int32, (tm, LANES), 0)
        onehot = jnp.where(lane - dh == (row // SLC_BLOCK) % (SEL_TK // SLC_BLOCK), 1.0, 0.0)
        for j in range(2):
            kj = pltpu.roll(slab, dh, axis=1) if j else slab
            ref[0, 2 * c + j] = jnp.where(lane < dh, kj, onehot).astype(ref.dtype)

    scale = dh ** -0.5 * LOG2E
    for c in range(_NQ // LANES):
        xn = seg_rms(p_ref[:, c * LANES:(c + 1) * LANES], qw_ref[...])
        put_t(qn_ref, xn * scale, c)
        put_t(qr_ref, rope(xn) * scale, c)
    for c in range(_KVW // LANES):
        put_aug(ks_ref, rope(seg_rms(p_ref[:, _OFF_KS + c * LANES:_OFF_KS + (c + 1) * LANES], ksw_ref[...])), c)
        put(kw_ref, rope(seg_rms(p_ref[:, _OFF_KW + c * LANES:_OFF_KW + (c + 1) * LANES], kww_ref[...])), c)
        put_t_tiles(vs_ref, p_ref[:, _OFF_VS + c * LANES:_OFF_VS + (c + 1) * LANES], c)
        put_t_tiles(vw_ref, p_ref[:, _OFF_VW + c * LANES:_OFF_VW + (c + 1) * LANES], c)
        put(kc_ref, p_ref[:, _OFF_KC + c * LANES:_OFF_KC + (c + 1) * LANES], c)
        put(vc_ref, p_ref[:, _OFF_VC + c * LANES:_OFF_VC + (c + 1) * LANES], c)
    g_ref[0] = jax.nn.sigmoid(p_ref[:, _OFF_G:_OFF_G + LANES]).T[0:g_ref.shape[1]]


def _nsa_prep(proj, pos, inv, qw, ksw, kww, bd, B, S, tm=SEL_TK):
    H, Hk, dh = NSA_HEADS, NSA_KV_HEADS, NSA_DH
    nt = S // tm
    NP = proj.shape[1]

    def pm(nh, dt):
        return (pl.BlockSpec((1, nh, tm, dh), lambda b, i: (b, 0, i, 0)),
                jax.ShapeDtypeStruct((B, nh, S, dh), dt))

    def dm(nh):
        return (pl.BlockSpec((1, nh, dh, tm), lambda b, i: (b, 0, 0, i)),
                jax.ShapeDtypeStruct((B, nh, dh, S), BF16))

    def dmt(nh):
        return (pl.BlockSpec((1, nh, tm // KT, dh, KT), lambda b, i: (b, 0, i, 0, 0)),
                jax.ShapeDtypeStruct((B, nh, S // KT, dh, KT), BF16))

    gr = Hk * GATE_ROWS
    aug = (pl.BlockSpec((1, Hk, tm, LANES), lambda b, i: (b, 0, i, 0)), jax.ShapeDtypeStruct((B, Hk, S, LANES), BF16))
    outs = [dm(H), dm(H), aug, pm(Hk, BF16), dmt(Hk), dmt(Hk), pm(Hk, F32), pm(Hk, F32),
            (pl.BlockSpec((1, gr, tm), lambda b, i: (b, 0, i)), jax.ShapeDtypeStruct((B, gr, S), F32))]
    return pl.pallas_call(
        _nsa_prep_kernel,
        grid=(B, nt),
        in_specs=[pl.BlockSpec((tm, NP), lambda b, i: (b * nt + i, 0)),
                  pl.BlockSpec((tm, 1), lambda b, i: (b * nt + i, 0)),
                  _resident(inv.shape), _resident(qw.shape), _resident(ksw.shape), _resident(kww.shape),
                  _resident(bd.shape)],
        out_specs=[o[0] for o in outs],
        out_shape=[o[1] for o in outs],
        compiler_params=_cparams(("parallel", "parallel")),
        name="nsa_prep",
    )(proj, pos, inv, qw, ksw, kww, bd)


def _gelu_tanh(x):
    return 0.5 * x * (1.0 + jnp.tanh(np.float32(np.sqrt(2.0 / np.pi)) * (x + 0.044715 * (x * x * x))))


def _cmp_kernel(kc_ref, vc_ref, pe_ref, w1_ref, b1_ref, w2k_ref, b2k_ref, w2vt_ref, b2v_ref, knw_ref, kco_ref, vco_ref):
    nch = kc_ref.shape[2]

    def hidden(src, i):
        x = src[0, 0]
        p1 = _dot(_bf(x + pe_ref[i, 0:1]), w1_ref[i, 0])
        p2 = _dot(_bf(x + pe_ref[i, 1:2]), w1_ref[i, 1])
        return _bf(_gelu_tanh(p1 + pltpu.roll(p2, nch - 1, axis=0) + b1_ref[i]))

    kcc = _dot(hidden(kc_ref, 0), w2k_ref[...]) + b2k_ref[...]
    kco_ref[0, 0] = _bf(_rms(kcc, knw_ref[...]))
    vco_ref[0, 0] = _bf(_dot_nt(w2vt_ref[...], hidden(vc_ref, 1)) + b2v_ref[...])


def _nsa_compress(kc, vc, pe, w1, b1, w2k, b2k, w2vt, b2v, knw):
    B, Hk, nch, cw = kc.shape
    dh = NSA_DH
    spec = pl.BlockSpec((1, 1, nch, cw), lambda b, h: (b, h, 0, 0))
    return pl.pallas_call(
        _cmp_kernel,
        grid=(B, Hk),
        in_specs=[spec, spec] + [_resident(a.shape) for a in (pe, w1, b1, w2k, b2k, w2vt, b2v, knw)],
        out_specs=[pl.BlockSpec((1, 1, nch, dh), lambda b, h: (b, h, 0, 0)),
                   pl.BlockSpec((1, 1, dh, nch), lambda b, h: (b, h, 0, 0))],
        out_shape=[jax.ShapeDtypeStruct((B, Hk, nch, dh), BF16), jax.ShapeDtypeStruct((B, Hk, dh, nch), BF16)],
        compiler_params=_cparams(("parallel", "parallel")),
        name="nsa_compress",
    )(kc, vc, pe, w1, b1, w2k, b2k, w2vt, b2v, knw)


def _nsa_attn_kernel(qn_ref, qr_ref, kcc_ref, vcct_ref, ovlt_ref, ks_ref, vst_ref, kw_ref, vwt_ref, g_ref, o_ref,
                     selb_ref, *, nsel):
    G, dh, tq = qn_ref.shape[2:]
    nck = kcc_ref.shape[2]
    ns = ovlt_ref.shape[0]
    tk = SEL_TK
    assert tk // SLC_BLOCK == 8
    s0 = pl.program_id(2) * tq
    tpos = s0 + lax.broadcasted_iota(jnp.int32, (1, tq), 1)
    heads = range(G)

    kcc = kcc_ref[0, 0]
    cend = lax.broadcasted_iota(jnp.int32, (nck, 1), 0) * CMP_STRIDE + (CMP_BLOCK - 1)
    cbias = jnp.where(cend <= tpos, 0.0, NEG)
    has_c = jnp.where(tpos >= CMP_BLOCK - 1, 1.0, 0.0)
    sc = [_dot(kcc, qn_ref[0, 0, g]) + cbias for g in heads]
    ec = [jnp.exp2(sc[g] - jnp.max(sc[g], axis=0, keepdims=True)) for g in heads]
    pc = [ec[g] * (has_c / jnp.sum(ec[g], axis=0, keepdims=True)) for g in heads]
    oc = [_dot(vcct_ref[0, 0], _bf(pc[g])) for g in heads]
    psum = pc[0]
    for g in range(1, G):
        psum = psum + pc[g]
    imp = _dot(ovlt_ref[...], _bf(psum))

    blk = lax.broadcasted_iota(jnp.int32, (ns, tq), 0)
    svalid = blk * SLC_BLOCK <= tpos
    dist = tpos // SLC_BLOCK - blk
    forced = (blk == 0) | ((dist >= 0) & (dist < N_LOCAL))
    score = jnp.where(svalid & forced, FORCE, jnp.where(svalid, imp, -1.0))
    ngrp = ns // 8
    sg = [score[8 * r:8 * r + 8] for r in range(ngrp)]
    bg = [blk[8 * r:8 * r + 8] for r in range(ngrp)]
    rank = [jnp.zeros((8, tq), F32) for _ in range(ngrp)]
    for j in range(ns):
        rowj = score[j:j + 1, :]
        for r in range(ngrp):
            if 8 * r + 7 <= j:
                beat = jnp.where(rowj > sg[r], 1.0, 0.0)
            elif 8 * r > j:
                beat = jnp.where(sg[r] > rowj, 0.0, 1.0)
            else:
                beat = jnp.where(bg[r] > j, jnp.where(sg[r] > rowj, 0.0, 1.0), jnp.where(rowj > sg[r], 1.0, 0.0))
            rank[r] = rank[r] + beat
    zpad = jnp.zeros((8, tq), F32)
    for t in range(ngrp):
        selb = jnp.where((rank[t] < nsel) & (sg[t] >= 0.0), 0.0, NEG)
        selb_ref[t] = _bf(jnp.concatenate([selb, zpad], axis=0))

    nsub = tk // KT
    qr = [qr_ref[0, 0, g] for g in heads]
    qpad = jnp.zeros((LANES - dh - 16, tq), BF16)

    def tile_scores(kt):
        kaug = ks_ref[0, 0, pl.ds(pl.multiple_of(kt * tk, tk), tk), :]
        sb = selb_ref[kt]
        return tuple(_dot(kaug, jnp.concatenate([qr[g], sb, qpad], axis=0)) for g in heads)

    def update(stats, s, kt):
        m, l, acc = stats
        vt = jnp.concatenate([vst_ref[0, 0, kt * nsub + j] for j in range(nsub)], axis=1)
        mn = [jnp.maximum(m[g], jnp.max(s[g], axis=0, keepdims=True)) for g in heads]
        p = [jnp.exp2(s[g] - mn[g]) for g in heads]
        alpha = [jnp.exp2(m[g] - mn[g]) for g in heads]
        l = [alpha[g] * l[g] + jnp.sum(p[g], axis=0, keepdims=True) for g in heads]
        acc = [alpha[g] * acc[g] + _dot(vt, _bf(p[g])) for g in heads]
        return tuple(mn), tuple(l), tuple(acc)

    def body(kt, stats):
        return update(stats, tile_scores(kt), kt)

    init = (tuple(jnp.full((1, tq), NEG, F32) for _ in heads),
            tuple(jnp.zeros((1, tq), F32) for _ in heads),
            tuple(jnp.zeros((dh, tq), F32) for _ in heads))
    last = (s0 + tq - 1) // tk
    stats = lax.fori_loop(0, last, body, init)
    s = tile_scores(last)
    kpos = last * tk + lax.broadcasted_iota(jnp.int32, (tk, 1), 0)
    causal = jnp.where(kpos <= tpos, 0.0, NEG)
    _, l, acc = update(stats, [s[g] + causal for g in heads], last)
    osel = [acc[g] * (1.0 / l[g]) for g in heads]

    ksub = lax.broadcasted_iota(jnp.int32, (KT, 1), 0)
    sw = [[] for _ in heads]
    vts = []
    for j in range((WINDOW + tq) // KT):
        start = s0 - WINDOW + j * KT
        st = pl.multiple_of(jnp.maximum(start, 0), KT)
        kpos = jnp.where(start >= 0, st, S_FAR) + ksub
        wbias = jnp.where((kpos <= tpos) & (kpos > tpos - WINDOW), 0.0, NEG)
        kj = kw_ref[0, 0, pl.ds(st, KT), :]
        vts.append(vwt_ref[0, 0, st // KT])
        for g in heads:
            sw[g].append(_dot(kj, qr[g]) + wbias)
    ow = []
    for g in heads:
        mw = sw[g][0].max(axis=0, keepdims=True)
        for j in range(1, len(vts)):
            mw = jnp.maximum(mw, sw[g][j].max(axis=0, keepdims=True))
        ew = [jnp.exp2(sw[g][j] - mw) for j in range(len(vts))]
        lw = ew[0].sum(axis=0, keepdims=True)
        o = _dot(vts[0], _bf(ew[0]))
        for j in range(1, len(vts)):
            lw = lw + ew[j].sum(axis=0, keepdims=True)
            o = o + _dot(vts[j], _bf(ew[j]))
        ow.append(o * (1.0 / lw))

    gt = g_ref[0]
    outs = [gt[3 * g:3 * g + 1] * oc[g] + gt[3 * g + 1:3 * g + 2] * osel[g] + gt[3 * g + 2:3 * g + 3] * ow[g]
            for g in heads]
    for c in range(G // 2):
        o_ref[:, c * LANES:(c + 1) * LANES] = _bf(jnp.concatenate([outs[2 * c], outs[2 * c + 1]], axis=0).T)


def _nsa_attn(qn, qr, kcc, vcct, ovlt, ks, vst, kw, vwt, gates, B, S, tq=128):
    Hk, G, dh = NSA_KV_HEADS, NSA_GROUP, NSA_DH
    nt = S // tq
    nck = kcc.shape[2]
    ns = ovlt.shape[0]
    q5 = lambda q: q.reshape(B, Hk, G, dh, S)
    qspec = pl.BlockSpec((1, 1, G, dh, tq), lambda b, h, i: (b, h, 0, 0, i))
    vspec = pl.BlockSpec((1, 1, S // KT, dh, KT), lambda b, h, i: (b, h, 0, 0, 0))
    return pl.pallas_call(
        functools.partial(_nsa_attn_kernel, nsel=min(SLC_TOPK, ns)),
        grid=(B, Hk, nt),
        in_specs=[qspec, qspec,
                  pl.BlockSpec((1, 1, nck, dh), lambda b, h, i: (b, h, 0, 0)),
                  pl.BlockSpec((1, 1, dh, nck), lambda b, h, i: (b, h, 0, 0)),
                  _resident(ovlt.shape),
                  pl.BlockSpec((1, 1, S, LANES), lambda b, h, i: (b, h, 0, 0)), vspec,
                  pl.BlockSpec((1, 1, S, dh), lambda b, h, i: (b, h, 0, 0)), vspec,
                  pl.BlockSpec((1, GATE_ROWS, tq), lambda b, h, i: (b, h, i))],
        out_specs=pl.BlockSpec((tq, G * dh), lambda b, h, i: (b * nt + i, h)),
        out_shape=jax.ShapeDtypeStruct((B * S, Hk * G * dh), BF16),
        scratch_shapes=[pltpu.VMEM((S // SEL_TK, 16, tq), BF16)],
        compiler_params=_cparams(("parallel", "parallel", "parallel")),
        name="nsa_attn",
    )(q5(qn), q5(qr), kcc, vcct, ovlt, ks, vst, kw, vwt, gates)


def _nsa_constants(S):
    dh = NSA_DH
    half = dh // 2
    inv = 1.0 / (ROPE_THETA ** (np.arange(0, dh, 2, dtype=np.float32) / dh))
    inv = np.tile(inv.astype(np.float32), LANES // half).reshape(1, LANES)
    lane = np.arange(LANES)
    bd = (lane[:, None] // dh == lane[None, :] // dh).astype(np.float32)
    nck = S // CMP_STRIDE
    ns = S // SLC_BLOCK
    ci = np.arange(nck)[None, :]
    sj = np.arange(ns)[:, None]
    ovlt = np.clip(np.minimum(ci * CMP_STRIDE + CMP_BLOCK, (sj + 1) * SLC_BLOCK)
                   - np.maximum(ci * CMP_STRIDE, sj * SLC_BLOCK), 0, None).astype(np.float32) / CMP_BLOCK
    return jnp.asarray(inv), jnp.asarray(bd, BF16), jnp.asarray(ovlt, BF16)


def _nsa_mixer(h2, nw, pos, w_in, q_norm, k_norm, cmp_pe, cmp_w1, cmp_b1, cmp_w2, cmp_b2, w_out, B, S):
    H, Hk, G, dh = NSA_HEADS, NSA_KV_HEADS, NSA_GROUP, NSA_DH
    inv, bd, ovlt = _nsa_constants(S)
    D = w_in.shape[0]
    wg = jnp.pad(w_in[:, _OFF_G:].reshape(D, Hk, 3 * G), ((0, 0), (0, 0), (0, GATE_ROWS - 3 * G)))
    wg = jnp.pad(wg.reshape(D, Hk * GATE_ROWS), ((0, 0), (0, LANES - Hk * GATE_ROWS)))
    w_in_p = jnp.concatenate([w_in[:, :_OFF_G], wg], axis=1).astype(BF16)
    proj = _norm_matmul(h2, nw, w_in_p, tn=w_in_p.shape[1] // 3)
    tile2 = lambda w: jnp.tile(w, LANES // dh).reshape(1, LANES)
    qn, qr, ks, kw, vst, vwt, kc, vc, gates = _nsa_prep(
        proj, pos.reshape(B * S, 1), inv, tile2(q_norm), tile2(k_norm[1]), tile2(k_norm[2]), bd, B, S)

    nch = S // CMP_STRIDE
    cw = CMP_STRIDE * dh
    kcc, vcct = _nsa_compress(
        kc.reshape(B, Hk, nch, cw), vc.reshape(B, Hk, nch, cw),
        cmp_pe.reshape(2, 2, cw), cmp_w1.reshape(2, 2, cw, CMP_HIDDEN).astype(BF16),
        cmp_b1.reshape(2, 1, CMP_HIDDEN), cmp_w2[0].astype(BF16), cmp_b2[0].reshape(1, dh),
        cmp_w2[1].T.astype(BF16), cmp_b2[1].reshape(dh, 1), k_norm[0].reshape(1, dh))

    o = _nsa_attn(qn, qr, kcc, vcct, ovlt, ks, vst, kw, vwt, gates, B, S)
    return _matmul_res(o, w_out.astype(BF16), h2)


def kernel(x, positions, ffn_norm, ffn_w_gate_up, ffn_w_down, mixer_norm, gdn_w_in, gdn_conv_w, gdn_A_log, gdn_dt_bias, gdn_out_norm, gdn_w_out, sc_w_in, sc_conv_w, sc_w_out, nsa_w_in, nsa_q_norm, nsa_k_norm, nsa_cmp_pe, nsa_cmp_w1, nsa_cmp_b1, nsa_cmp_w2, nsa_cmp_b2, nsa_w_out):
    B, S, D = x.shape
    depth = ffn_norm.shape[0]
    h = x.reshape(B * S, D)
    for layer in range(depth):
        h = _ffn(h, ffn_norm[layer, 0].reshape(1, D), *_prep_ffn_weights(ffn_w_gate_up[layer, 0], ffn_w_down[layer, 0]))
        nw = mixer_norm[layer].reshape(1, D)
        kind = layer % N_MIXERS
        j = layer // N_MIXERS
        if kind == 0:
            h = _gdn_mixer(h, nw, gdn_w_in[j], gdn_conv_w[j], gdn_A_log[j], gdn_dt_bias[j],
                           gdn_out_norm[j], gdn_w_out[j], B, S)
        elif kind == 1:
            h = _sc_mixer(h, nw, sc_w_in[j], sc_conv_w[j], sc_w_out[j], B, S)
        else:
            h = _nsa_mixer(h, nw, positions, nsa_w_in[j], nsa_q_norm[j], nsa_k_norm[j], nsa_cmp_pe[j],
                           nsa_cmp_w1[j], nsa_cmp_b1[j], nsa_cmp_w2[j], nsa_cmp_b2[j], nsa_w_out[j], B, S)
        h = _ffn(h, ffn_norm[layer, 1].reshape(1, D), *_prep_ffn_weights(ffn_w_gate_up[layer, 1], ffn_w_down[layer, 1]))
    return h.reshape(B, S, D)
```

```python
import functools

import numpy as np
import jax
import jax.numpy as jnp
from jax import lax
from jax.experimental import pallas as pl
from jax.experimental.pallas import tpu as pltpu

F32 = jnp.float32
BF16 = jnp.bfloat16

EPS = 1e-6
NEG = -1e30
FORCE = 1e9
S_FAR = 1 << 30
LANES = 128

GDN_HEADS = 8
GDN_DK = 128
GDN_DV = 128
GDN_CONV = 4
GDN_CHUNK = 64
SC_WIDTH = 3
NSA_HEADS = 16
NSA_KV_HEADS = 4
NSA_GROUP = NSA_HEADS // NSA_KV_HEADS
NSA_DH = 64
CMP_BLOCK = 32
CMP_STRIDE = 16
CMP_HIDDEN = 256
SLC_BLOCK = 64
SLC_TOPK = 16
N_LOCAL = 2
WINDOW = 512
ROPE_THETA = 10000.0
N_MIXERS = 3
GATE_ROWS = 16

VMEM_LIMIT = 56 * 1024 * 1024


def _cparams(sem):
    return pltpu.CompilerParams(dimension_semantics=sem, vmem_limit_bytes=VMEM_LIMIT)


def _resident(shape):
    nd = len(shape)
    return pl.BlockSpec(shape, lambda *_: (0,) * nd, pipeline_mode=pl.Buffered(1))


def _rms(x, w):
    return x * lax.rsqrt(jnp.mean(x * x, axis=-1, keepdims=True) + EPS) * w


def _silu(x):
    return x * jax.nn.sigmoid(x)


def _softplus(x):
    return jnp.maximum(x, 0.0) + jnp.log1p(jnp.exp(-jnp.abs(x)))


def _bf(x):
    return x.astype(BF16)


def _dot(a, b):
    return jnp.dot(a, b, preferred_element_type=F32)


def _dot_nt(a, b):
    return lax.dot_general(a, b, (((1,), (1,)), ((), ())), preferred_element_type=F32)


def _split(x):
    hi = x.astype(BF16)
    lo = (x - hi.astype(F32)).astype(BF16)
    return hi, lo


def _dot2_lhs(a, b_bf16):
    ah, al = _split(a)
    return _dot(ah, b_bf16) + _dot(al, b_bf16)


def _ffn_kernel(x_ref, nw_ref, wg_ref, wu_ref, wd_ref, o_ref, a_ref, *, tf):
    x = x_ref[...]
    xn = _bf(_rms(x, nw_ref[...]))
    for c in range(wg_ref.shape[1] // tf):
        cols = slice(c * tf, (c + 1) * tf)
        a_ref[:, cols] = _bf(_silu(_dot(xn, wg_ref[:, cols])) * _dot(xn, wu_ref[:, cols]))
    o_ref[...] = x + 0.5 * _dot(a_ref[...], wd_ref[...])


def _ffn(h2, nw, wg, wu, wd, tm=512, tf=256):
    T, D = h2.shape
    return pl.pallas_call(
        functools.partial(_ffn_kernel, tf=tf),
        grid=(T // tm,),
        in_specs=[pl.BlockSpec((tm, D), lambda i: (i, 0)),
                  _resident(nw.shape), _resident(wg.shape), _resident(wu.shape), _resident(wd.shape)],
        out_specs=pl.BlockSpec((tm, D), lambda i: (i, 0)),
        out_shape=jax.ShapeDtypeStruct((T, D), F32),
        scratch_shapes=[pltpu.VMEM((tm, wg.shape[1]), BF16)],
        compiler_params=_cparams(("parallel",)),
        name="ffn",
    )(h2, nw, wg, wu, wd)


def _prep_ffn_weights(w_gate_up, w_down):
    fh = w_gate_up.shape[1] // 2
    return w_gate_up[:, :fh].astype(BF16), w_gate_up[:, fh:].astype(BF16), w_down.astype(BF16)


def _norm_matmul_kernel(x_ref, nw_ref, w_ref, o_ref, *, tn):
    xn = _rms(x_ref[...], nw_ref[...]).astype(BF16)
    for c in range(w_ref.shape[1] // tn):
        o_ref[:, c * tn:(c + 1) * tn] = _dot(xn, w_ref[:, c * tn:(c + 1) * tn])


def _norm_matmul(h2, nw, w, tn, tm=256):
    T, D = h2.shape
    N = w.shape[1]
    return pl.pallas_call(
        functools.partial(_norm_matmul_kernel, tn=tn),
        grid=(T // tm,),
        in_specs=[pl.BlockSpec((tm, D), lambda i: (i, 0)), _resident(nw.shape), _resident(w.shape)],
        out_specs=pl.BlockSpec((tm, N), lambda i: (i, 0)),
        out_shape=jax.ShapeDtypeStruct((T, N), F32),
        compiler_params=_cparams(("parallel",)),
        name="norm_matmul",
    )(h2, nw, w)


def _matmul_res_kernel(a_ref, w_ref, h_ref, o_ref):
    o_ref[...] = h_ref[...] + _dot(a_ref[...], w_ref[...])


def _matmul_res(a, w, h2, tm=512):
    T, D = h2.shape
    K = a.shape[1]
    return pl.pallas_call(
        _matmul_res_kernel,
        grid=(T // tm,),
        in_specs=[pl.BlockSpec((tm, K), lambda i: (i, 0)), _resident(w.shape),
                  pl.BlockSpec((tm, D), lambda i: (i, 0))],
        out_specs=pl.BlockSpec((tm, D), lambda i: (i, 0)),
        out_shape=jax.ShapeDtypeStruct((T, D), F32),
        compiler_params=_cparams(("parallel",)),
        name="matmul_res",
    )(a, w, h2)


def _gdn_kernel(qkv_ref, gate_ref, ab_ref, cw_ref, alog_ref, dtb_ref, onw_ref, o_ref, xbuf_ref, s_ref, *, nch):
    C = GDN_CHUNK
    H, DK, DV = GDN_HEADS, GDN_DK, GDN_DV
    R = nch * C
    n = pl.program_id(1)

    @pl.when(n == 0)
    def _():
        xbuf_ref[0:8, :] = jnp.zeros((8, xbuf_ref.shape[1]), F32)
        s_ref[...] = jnp.zeros_like(s_ref)

    raw = qkv_ref[...]
    xbuf_ref[8:8 + R, :] = raw
    cw = cw_ref[...]
    y = (xbuf_ref[5:5 + R, :] * cw[0:1] + xbuf_ref[6:6 + R, :] * cw[1:2]
         + xbuf_ref[7:7 + R, :] * cw[2:3] + raw * cw[3:4])
    xbuf_ref[0:8, :] = raw[R - 8:R]
    y = _silu(y)

    ab = ab_ref[...]
    g_all = -jnp.exp(alog_ref[...]) * _softplus(ab + dtb_ref[...])
    beta_all = jax.nn.sigmoid(ab)
    rowc = lax.broadcasted_iota(jnp.int32, (R, LANES), 0) % C
    gc = g_all
    s = 1
    while s < C:
        gc = gc + jnp.where(rowc >= s, pltpu.roll(gc, s, axis=0), 0.0)
        s *= 2
    egc = jnp.exp(gc)
    gcc = [gc[c * C:(c + 1) * C] for c in range(nch)]
    gct = [g.T for g in gcc]
    glast = [g[C - 1:C, :] for g in gcc]
    ekd = [jnp.exp(glast[c] - gcc[c]) for c in range(nch)]
    egl = [jnp.exp(glast[c]) for c in range(nch)]

    ri = lax.broadcasted_iota(jnp.int32, (C, C), 0)
    ci = lax.broadcasted_iota(jnp.int32, (C, C), 1)
    lower = ri >= ci
    strict = ri > ci

    pairs = [(c, h) for c in range(nch) for h in range(H)]
    dec, qnb, knb, kbb, sol, qd, kdt = {}, {}, {}, {}, {}, {}, {}
    for p in pairs:
        c, h = p
        rows = slice(c * C, (c + 1) * C)
        gcol = gcc[c][:, h:h + 1]
        grow = gct[c][h:h + 1, :]
        beta = beta_all[rows, H + h:H + h + 1]
        e = egc[rows, h:h + 1]
        dec[p] = jnp.where(lower, jnp.exp(jnp.where(lower, gcol - grow, 0.0)), 0.0)
        q = y[rows, h * DK:(h + 1) * DK]
        k = y[rows, H * DK + h * DK:H * DK + (h + 1) * DK]
        v = y[rows, 2 * H * DK + h * DV:2 * H * DK + (h + 1) * DV]
        qn = q * lax.rsqrt(jnp.sum(q * q, -1, keepdims=True) + EPS) * (DK ** -0.5)
        kn = k * lax.rsqrt(jnp.sum(k * k, -1, keepdims=True) + EPS)
        kb = kn * beta
        qnb[p], knb[p], kbb[p] = _bf(qn), _bf(kn), _bf(kb)
        sol[p] = jnp.concatenate([v * beta, kb * e], axis=1)
        qd[p] = qn * e
        kdt[p] = _bf((kn * ekd[c][:, h:h + 1]).T)

    kk = {p: _dot_nt(kbb[p], knb[p]) for p in pairs}
    qk = {p: _dot_nt(qnb[p], knb[p]) for p in pairs}
    nil = {p: jnp.where(strict, -kk[p] * dec[p], 0.0) for p in pairs}
    attn = {p: _bf(jnp.where(lower, qk[p] * dec[p], 0.0)) for p in pairs}
    x = {p: _bf(nil[p]) for p in pairs}
    pw = 2
    while pw < C:
        x2 = {p: _dot(x[p], x[p]) for p in pairs}
        x = {p: _bf(x2[p]) for p in pairs}
        nil = {p: nil[p] + x2[p] + _dot(_bf(nil[p]), x[p]) for p in pairs}
        pw *= 2
    sol = {p: sol[p] + _dot(_bf(nil[p]), _bf(sol[p])) for p in pairs}

    st = [s_ref[h] for h in range(H)]
    for c in range(nch):
        r = [_dot(_bf(jnp.concatenate([sol[(c, h)][:, DV:], qd[(c, h)]], axis=0)), _bf(st[h])) for h in range(H)]
        vb = [_bf(sol[(c, h)][:, :DV] - r[h][:C]) for h in range(H)]
        o = [r[h][C:] + _dot(attn[(c, h)], vb[h]) for h in range(H)]
        st = [st[h] * egl[c][:, h:h + 1] + _dot(kdt[(c, h)], vb[h]) for h in range(H)]
        for h in range(H):
            gt = gate_ref[c * C:(c + 1) * C, h * DV:(h + 1) * DV]
            o_ref[c * C:(c + 1) * C, h * DV:(h + 1) * DV] = _bf(_rms(o[h], onw_ref[...]) * _silu(gt))
    for h in range(H):
        s_ref[h] = st[h]


def _gdn_core(proj, cw, alog, dtb, onw, B, S, nch=2):
    R = nch * GDN_CHUNK
    NQ = 2 * GDN_HEADS * GDN_DK + GDN_HEADS * GDN_DV
    NG = GDN_HEADS * GDN_DV
    nstep = S // R
    T = B * S
    return pl.pallas_call(
        functools.partial(_gdn_kernel, nch=nch),
        grid=(B, nstep),
        in_specs=[pl.BlockSpec((R, NQ), lambda b, n: (b * nstep + n, 0)),
                  pl.BlockSpec((R, NG), lambda b, n: (b * nstep + n, NQ // NG)),
                  pl.BlockSpec((R, LANES), lambda b, n: (b * nstep + n, (NQ + NG) // LANES)),
                  _resident(cw.shape), _resident(alog.shape), _resident(dtb.shape), _resident(onw.shape)],
        out_specs=pl.BlockSpec((R, NG), lambda b, n: (b * nstep + n, 0)),
        out_shape=jax.ShapeDtypeStruct((T, NG), BF16),
        scratch_shapes=[pltpu.VMEM((8 + R, NQ), F32), pltpu.VMEM((GDN_HEADS, GDN_DK, GDN_DV), F32)],
        compiler_params=_cparams(("parallel", "arbitrary")),
        name="gdn_core",
    )(proj, proj, proj, cw, alog, dtb, onw)


def _gdn_mixer(h2, nw, w_in, conv_w, a_log, dt_bias, out_norm, w_out, B, S):
    H = GDN_HEADS
    n_used = w_in.shape[1]
    n_pad = -n_used % LANES
    w_in_p = jnp.pad(w_in, ((0, 0), (0, n_pad))).astype(BF16)
    proj = _norm_matmul(h2, nw, w_in_p, tn=w_in_p.shape[1] // 3)
    lane_pad = LANES - H
    alog = jnp.pad(a_log, (0, lane_pad)).reshape(1, LANES)
    dtb = jnp.pad(dt_bias, (0, lane_pad)).reshape(1, LANES)
    o = _gdn_core(proj, conv_w, alog, dtb, out_norm.reshape(1, GDN_DV), B, S)
    return _matmul_res(o, w_out.astype(BF16), h2)


def _sc_kernel(h_ref, nw_ref, win_ref, cw_ref, wout_ref, o_ref, cx_ref):
    tm, D = h_ref.shape
    i = pl.program_id(1)

    @pl.when(i == 0)
    def _():
        cx_ref[0:8, :] = jnp.zeros((8, D), F32)

    x = h_ref[...]
    xn = _rms(x, nw_ref[...]).astype(BF16)
    bg = _dot(xn, win_ref[:, 0:D])
    cx = _dot(xn, win_ref[:, D:2 * D]) * _dot(xn, win_ref[:, 2 * D:3 * D])
    cx_ref[8:8 + tm, :] = cx
    cw = cw_ref[...]
    y = cx_ref[6:6 + tm, :] * cw[0:1] + cx_ref[7:7 + tm, :] * cw[1:2] + cx * cw[2:3]
    cx_ref[0:8, :] = cx[tm - 8:tm]
    o_ref[...] = x + _dot((bg * y).astype(BF16), wout_ref[...])


def _sc_mixer(h2, nw, w_in, conv_w, w_out, B, S, tm=256):
    T, D = h2.shape
    nt = S // tm
    return pl.pallas_call(
        _sc_kernel,
        grid=(B, nt),
        in_specs=[pl.BlockSpec((tm, D), lambda b, i: (b * nt + i, 0)),
                  _resident(nw.shape), _resident(w_in.shape), _resident(conv_w.shape), _resident(w_out.shape)],
        out_specs=pl.BlockSpec((tm, D), lambda b, i: (b * nt + i, 0)),
        out_shape=jax.ShapeDtypeStruct((T, D), F32),
        scratch_shapes=[pltpu.VMEM((8 + tm, D), F32)],
        compiler_params=_cparams(("parallel", "arbitrary")),
        name="short_conv",
    )(h2, nw, w_in.astype(BF16), conv_w, w_out.astype(BF16))


_NQ = NSA_HEADS * NSA_DH
_KVW = NSA_KV_HEADS * NSA_DH
_OFF_KC, _OFF_VC, _OFF_KS, _OFF_VS, _OFF_KW, _OFF_VW = (_NQ + i * _KVW for i in range(6))
_OFF_G = _NQ + 6 * _KVW
KT = 128
SEL_TK = 512
LOG2E = float(np.log2(np.e))


def _half(slab, j):
    if j:
        slab = pltpu.roll(slab, NSA_DH, axis=1)
    return slab[:, 0:NSA_DH]


def _nsa_prep_kernel(p_ref, pos_ref, inv_ref, qw_ref, ksw_ref, kww_ref, bd_ref,
                     qn_ref, qr_ref, ks_ref, kw_ref, vs_ref, vw_ref, kc_ref, vc_ref, g_ref):
    tm = p_ref.shape[0]
    dh = NSA_DH
    ang = pos_ref[...].astype(F32) * inv_ref[...]
    cos = jnp.cos(ang)
    sin = jnp.sin(ang)
    lane = lax.broadcasted_iota(jnp.int32, (tm, LANES), 1)
    first_half = (lane % dh) < (dh // 2)
    bd = bd_ref[...]

    def seg_rms(x, w):
        ss = _dot2_lhs(x * x, bd)
        return x * lax.rsqrt(ss * (1.0 / dh) + EPS) * w

    def rope(x):
        rot = jnp.where(first_half, -pltpu.roll(x, LANES - dh // 2, axis=1), pltpu.roll(x, dh // 2, axis=1))
        return x * cos + rot * sin

    def put(ref, slab, c):
        for j in range(2):
            ref[0, 2 * c + j] = _half(slab, j).astype(ref.dtype)

    def put_t(ref, slab, c):
        t = slab.T
        for j in range(2):
            ref[0, 2 * c + j] = t[j * dh:(j + 1) * dh].astype(ref.dtype)

    def put_t_tiles(ref, slab, c):
        t = slab.T
        for j in range(2):
            for tt in range(tm // KT):
                ref[0, 2 * c + j, tt] = t[j * dh:(j + 1) * dh, tt * KT:(tt + 1) * KT].astype(ref.dtype)

    def put_aug(ref, slab, c):
        row = lax.broadcasted_iota(jnp.int32, (tm, LANES), 0)
        onehot = jnp.where(lane - dh == (row // SLC_BLOCK) % (SEL_TK // SLC_BLOCK), 1.0, 0.0)
        for j in range(2):
            kj = pltpu.roll(slab, dh, axis=1) if j else slab
            ref[0, 2 * c + j] = jnp.where(lane < dh, kj, onehot).astype(ref.dtype)

    scale = dh ** -0.5 * LOG2E
    for c in range(_NQ // LANES):
        xn = seg_rms(p_ref[:, c * LANES:(c + 1) * LANES], qw_ref[...])
        put_t(qn_ref, xn * scale, c)
        put_t(qr_ref, rope(xn) * scale, c)
    for c in range(_KVW // LANES):
        put_aug(ks_ref, rope(seg_rms(p_ref[:, _OFF_KS + c * LANES:_OFF_KS + (c + 1) * LANES], ksw_ref[...])), c)
        put(kw_ref, rope(seg_rms(p_ref[:, _OFF_KW + c * LANES:_OFF_KW + (c + 1) * LANES], kww_ref[...])), c)
        put_t_tiles(vs_ref, p_ref[:, _OFF_VS + c * LANES:_OFF_VS + (c + 1) * LANES], c)
        put_t_tiles(vw_ref, p_ref[:, _OFF_VW + c * LANES:_OFF_VW + (c + 1) * LANES], c)
        put(kc_ref, p_ref[:, _OFF_KC + c * LANES:_OFF_KC + (c + 1) * LANES], c)
        put(vc_ref, p_ref[:, _OFF_VC + c * LANES:_OFF_VC + (c + 1) * LANES], c)
    g_ref[0] = jax.nn.sigmoid(p_ref[:, _OFF_G:_OFF_G + LANES]).T[0:g_ref.shape[1]]


def _nsa_prep(proj, pos, inv, qw, ksw, kww, bd, B, S, tm=SEL_TK):
    H, Hk, dh = NSA_HEADS, NSA_KV_HEADS, NSA_DH
    nt = S // tm
    NP = proj.shape[1]

    def pm(nh, dt):
        return (pl.BlockSpec((1, nh, tm, dh), lambda b, i: (b, 0, i, 0)),
                jax.ShapeDtypeStruct((B, nh, S, dh), dt))

    def dm(nh):
        return (pl.BlockSpec((1, nh, dh, tm), lambda b, i: (b, 0, 0, i)),
                jax.ShapeDtypeStruct((B, nh, dh, S), BF16))

    def dmt(nh):
        return (pl.BlockSpec((1, nh, tm // KT, dh, KT), lambda b, i: (b, 0, i, 0, 0)),
                jax.ShapeDtypeStruct((B, nh, S // KT, dh, KT), BF16))

    gr = Hk * GATE_ROWS
    aug = (pl.BlockSpec((1, Hk, tm, LANES), lambda b, i: (b, 0, i, 0)), jax.ShapeDtypeStruct((B, Hk, S, LANES), BF16))
    outs = [dm(H), dm(H), aug, pm(Hk, BF16), dmt(Hk), dmt(Hk), pm(Hk, F32), pm(Hk, F32),
            (pl.BlockSpec((1, gr, tm), lambda b, i: (b, 0, i)), jax.ShapeDtypeStruct((B, gr, S), F32))]
    return pl.pallas_call(
        _nsa_prep_kernel,
        grid=(B, nt),
        in_specs=[pl.BlockSpec((tm, NP), lambda b, i: (b * nt + i, 0)),
                  pl.BlockSpec((tm, 1), lambda b, i: (b * nt + i, 0)),
                  _resident(inv.shape), _resident(qw.shape), _resident(ksw.shape), _resident(kww.shape),
                  _resident(bd.shape)],
        out_specs=[o[0] for o in outs],
        out_shape=[o[1] for o in outs],
        compiler_params=_cparams(("parallel", "parallel")),
        name="nsa_prep",
    )(proj, pos, inv, qw, ksw, kww, bd)


def _gelu_tanh(x):
    return 0.5 * x * (1.0 + jnp.tanh(np.float32(np.sqrt(2.0 / np.pi)) * (x + 0.044715 * (x * x * x))))


def _cmp_kernel(kc_ref, vc_ref, pe_ref, w1_ref, b1_ref, w2k_ref, b2k_ref, w2vt_ref, b2v_ref, knw_ref, kco_ref, vco_ref):
    nch = kc_ref.shape[2]

    def hidden(src, i):
        x = src[0, 0]
        p1 = _dot(_bf(x + pe_ref[i, 0:1]), w1_ref[i, 0])
        p2 = _dot(_bf(x + pe_ref[i, 1:2]), w1_ref[i, 1])
        return _bf(_gelu_tanh(p1 + pltpu.roll(p2, nch - 1, axis=0) + b1_ref[i]))

    kcc = _dot(hidden(kc_ref, 0), w2k_ref[...]) + b2k_ref[...]
    kco_ref[0, 0] = _bf(_rms(kcc, knw_ref[...]))
    vco_ref[0, 0] = _bf(_dot_nt(w2vt_ref[...], hidden(vc_ref, 1)) + b2v_ref[...])


def _nsa_compress(kc, vc, pe, w1, b1, w2k, b2k, w2vt, b2v, knw):
    B, Hk, nch, cw = kc.shape
    dh = NSA_DH
    spec = pl.BlockSpec((1, 1, nch, cw), lambda b, h: (b, h, 0, 0))
    return pl.pallas_call(
        _cmp_kernel,
        grid=(B, Hk),
        in_specs=[spec, spec] + [_resident(a.shape) for a in (pe, w1, b1, w2k, b2k, w2vt, b2v, knw)],
        out_specs=[pl.BlockSpec((1, 1, nch, dh), lambda b, h: (b, h, 0, 0)),
                   pl.BlockSpec((1, 1, dh, nch), lambda b, h: (b, h, 0, 0))],
        out_shape=[jax.ShapeDtypeStruct((B, Hk, nch, dh), BF16), jax.ShapeDtypeStruct((B, Hk, dh, nch), BF16)],
        compiler_params=_cparams(("parallel", "parallel")),
        name="nsa_compress",
    )(kc, vc, pe, w1, b1, w2k, b2k, w2vt, b2v, knw)


def _nsa_attn_kernel(qn_ref, qr_ref, kcc_ref, vcct_ref, ovlt_ref, ks_ref, vst_ref, kw_ref, vwt_ref, g_ref, o_ref,
                     selb_ref, *, nsel):
    G, dh, tq = qn_ref.shape[2:]
    nck = kcc_ref.shape[2]
    ns = ovlt_ref.shape[0]
    tk = SEL_TK
    assert tk // SLC_BLOCK == 8
    s0 = pl.program_id(2) * tq
    tpos = s0 + lax.broadcasted_iota(jnp.int32, (1, tq), 1)
    heads = range(G)

    kcc = kcc_ref[0, 0]
    cend = lax.broadcasted_iota(jnp.int32, (nck, 1), 0) * CMP_STRIDE + (CMP_BLOCK - 1)
    cbias = jnp.where(cend <= tpos, 0.0, NEG)
    has_c = jnp.where(tpos >= CMP_BLOCK - 1, 1.0, 0.0)
    sc = [_dot(kcc, qn_ref[0, 0, g]) + cbias for g in heads]
    ec = [jnp.exp2(sc[g] - jnp.max(sc[g], axis=0, keepdims=True)) for g in heads]
    pc = [ec[g] * (has_c / jnp.sum(ec[g], axis=0, keepdims=True)) for g in heads]
    oc = [_dot(vcct_ref[0, 0], _bf(pc[g])) for g in heads]
    psum = pc[0]
    for g in range(1, G):
        psum = psum + pc[g]
    imp = _dot(ovlt_ref[...], _bf(psum))

    blk = lax.broadcasted_iota(jnp.int32, (ns, tq), 0)
    svalid = blk * SLC_BLOCK <= tpos
    dist = tpos // SLC_BLOCK - blk
    forced = (blk == 0) | ((dist >= 0) & (dist < N_LOCAL))
    score = jnp.where(svalid & forced, FORCE, jnp.where(svalid, imp, -1.0))
    ngrp = ns // 8
    selb = [[] for _ in range(ngrp)]
    for w in range(tq // LANES):
        sw_ = score[:, w * LANES:(w + 1) * LANES]
        sg = [sw_[8 * r:8 * r + 8] for r in range(ngrp)]
        bg = [blk[8 * r:8 * r + 8, 0:LANES] for r in range(ngrp)]
        rank = [jnp.zeros((8, LANES), F32) for _ in range(ngrp)]
        for j in range(ns):
            rowj = sw_[j:j + 1, :]
            for r in range(ngrp):
                if 8 * r + 7 <= j:
                    beat = jnp.where(rowj > sg[r], 1.0, 0.0)
                elif 8 * r > j:
                    beat = jnp.where(sg[r] > rowj, 0.0, 1.0)
                else:
                    beat = jnp.where(bg[r] > j, jnp.where(sg[r] > rowj, 0.0, 1.0),
                                     jnp.where(rowj > sg[r], 1.0, 0.0))
                rank[r] = rank[r] + beat
        for r in range(ngrp):
            selb[r].append(jnp.where((rank[r] < nsel) & (sg[r] >= 0.0), 0.0, NEG))
    zpad = jnp.zeros((8, tq), F32)
    for t in range(ngrp):
        selb_ref[t] = _bf(jnp.concatenate([jnp.concatenate(selb[t], axis=1), zpad], axis=0))

    nsub = tk // KT
    qr = [qr_ref[0, 0, g] for g in heads]
    qpad = jnp.zeros((LANES - dh - 16, tq), BF16)

    def tile_scores(kt):
        kaug = ks_ref[0, 0, pl.ds(pl.multiple_of(kt * tk, tk), tk), :]
        sb = selb_ref[kt]
        return tuple(_dot(kaug, jnp.concatenate([qr[g], sb, qpad], axis=0)) for g in heads)

    def update(stats, s, kt):
        m, l, acc = stats
        vt = jnp.concatenate([vst_ref[0, 0, kt * nsub + j] for j in range(nsub)], axis=1)
        mn = [jnp.maximum(m[g], jnp.max(s[g], axis=0, keepdims=True)) for g in heads]
        p = [jnp.exp2(s[g] - mn[g]) for g in heads]
        alpha = [jnp.exp2(m[g] - mn[g]) for g in heads]
        l = [alpha[g] * l[g] + jnp.sum(p[g], axis=0, keepdims=True) for g in heads]
        acc = [alpha[g] * acc[g] + _dot(vt, _bf(p[g])) for g in heads]
        return tuple(mn), tuple(l), tuple(acc)

    def body(kt, stats):
        return update(stats, tile_scores(kt), kt)

    init = (tuple(jnp.full((1, tq), NEG, F32) for _ in heads),
            tuple(jnp.zeros((1, tq), F32) for _ in heads),
            tuple(jnp.zeros((dh, tq), F32) for _ in heads))
    last = (s0 + tq - 1) // tk
    stats = lax.fori_loop(0, last, body, init)
    s = tile_scores(last)
    kpos = last * tk + lax.broadcasted_iota(jnp.int32, (tk, 1), 0)
    causal = jnp.where(kpos <= tpos, 0.0, NEG)
    _, l, acc = update(stats, [s[g] + causal for g in heads], last)
    osel = [acc[g] * (1.0 / l[g]) for g in heads]

    ksub = lax.broadcasted_iota(jnp.int32, (KT, 1), 0)
    sw = [[] for _ in heads]
    vts = []
    for j in range((WINDOW + tq) // KT):
        start = s0 - WINDOW + j * KT
        st = pl.multiple_of(jnp.maximum(start, 0), KT)
        kpos = jnp.where(start >= 0, st, S_FAR) + ksub
        wbias = jnp.where((kpos <= tpos) & (kpos > tpos - WINDOW), 0.0, NEG)
        kj = kw_ref[0, 0, pl.ds(st, KT), :]
        vts.append(vwt_ref[0, 0, st // KT])
        for g in heads:
            sw[g].append(_dot(kj, qr[g]) + wbias)
    ow = []
    for g in heads:
        mw = sw[g][0].max(axis=0, keepdims=True)
        for j in range(1, len(vts)):
            mw = jnp.maximum(mw, sw[g][j].max(axis=0, keepdims=True))
        ew = [jnp.exp2(sw[g][j] - mw) for j in range(len(vts))]
        lw = ew[0].sum(axis=0, keepdims=True)
        o = _dot(vts[0], _bf(ew[0]))
        for j in range(1, len(vts)):
            lw = lw + ew[j].sum(axis=0, keepdims=True)
            o = o + _dot(vts[j], _bf(ew[j]))
        ow.append(o * (1.0 / lw))

    gt = g_ref[0]
    outs = [gt[3 * g:3 * g + 1] * oc[g] + gt[3 * g + 1:3 * g + 2] * osel[g] + gt[3 * g + 2:3 * g + 3] * ow[g]
            for g in heads]
    for c in range(G // 2):
        o_ref[:, c * LANES:(c + 1) * LANES] = _bf(jnp.concatenate([outs[2 * c], outs[2 * c + 1]], axis=0).T)


def _nsa_attn(qn, qr, kcc, vcct, ovlt, ks, vst, kw, vwt, gates, B, S, tq=256):
    Hk, G, dh = NSA_KV_HEADS, NSA_GROUP, NSA_DH
    nt = S // tq
    nck = kcc.shape[2]
    ns = ovlt.shape[0]
    q5 = lambda q: q.reshape(B, Hk, G, dh, S)
    qspec = pl.BlockSpec((1, 1, G, dh, tq), lambda b, h, i: (b, h, 0, 0, i))
    vspec = pl.BlockSpec((1, 1, S // KT, dh, KT), lambda b, h, i: (b, h, 0, 0, 0))
    return pl.pallas_call(
        functools.partial(_nsa_attn_kernel, nsel=min(SLC_TOPK, ns)),
        grid=(B, Hk, nt),
        in_specs=[qspec, qspec,
                  pl.BlockSpec((1, 1, nck, dh), lambda b, h, i: (b, h, 0, 0)),
                  pl.BlockSpec((1, 1, dh, nck), lambda b, h, i: (b, h, 0, 0)),
                  _resident(ovlt.shape),
                  pl.BlockSpec((1, 1, S, LANES), lambda b, h, i: (b, h, 0, 0)), vspec,
                  pl.BlockSpec((1, 1, S, dh), lambda b, h, i: (b, h, 0, 0)), vspec,
                  pl.BlockSpec((1, GATE_ROWS, tq), lambda b, h, i: (b, h, i))],
        out_specs=pl.BlockSpec((tq, G * dh), lambda b, h, i: (b * nt + i, h)),
        out_shape=jax.ShapeDtypeStruct((B * S, Hk * G * dh), BF16),
        scratch_shapes=[pltpu.VMEM((S // SEL_TK, 16, tq), BF16)],
        compiler_params=_cparams(("parallel", "parallel", "parallel")),
        name="nsa_attn",
    )(q5(qn), q5(qr), kcc, vcct, ovlt, ks, vst, kw, vwt, gates)


def _nsa_constants(S):
    dh = NSA_DH
    half = dh // 2
    inv = 1.0 / (ROPE_THETA ** (np.arange(0, dh, 2, dtype=np.float32) / dh))
    inv = np.tile(inv.astype(np.float32), LANES // half).reshape(1, LANES)
    lane = np.arange(LANES)
    bd = (lane[:, None] // dh == lane[None, :] // dh).astype(np.float32)
    nck = S // CMP_STRIDE
    ns = S // SLC_BLOCK
    ci = np.arange(nck)[None, :]
    sj = np.arange(ns)[:, None]
    ovlt = np.clip(np.minimum(ci * CMP_STRIDE + CMP_BLOCK, (sj + 1) * SLC_BLOCK)
                   - np.maximum(ci * CMP_STRIDE, sj * SLC_BLOCK), 0, None).astype(np.float32) / CMP_BLOCK
    return jnp.asarray(inv), jnp.asarray(bd, BF16), jnp.asarray(ovlt, BF16)


def _nsa_mixer(h2, nw, pos, w_in, q_norm, k_norm, cmp_pe, cmp_w1, cmp_b1, cmp_w2, cmp_b2, w_out, B, S):
    H, Hk, G, dh = NSA_HEADS, NSA_KV_HEADS, NSA_GROUP, NSA_DH
    inv, bd, ovlt = _nsa_constants(S)
    D = w_in.shape[0]
    wg = jnp.pad(w_in[:, _OFF_G:].reshape(D, Hk, 3 * G), ((0, 0), (0, 0), (0, GATE_ROWS - 3 * G)))
    wg = jnp.pad(wg.reshape(D, Hk * GATE_ROWS), ((0, 0), (0, LANES - Hk * GATE_ROWS)))
    w_in_p = jnp.concatenate([w_in[:, :_OFF_G], wg], axis=1).astype(BF16)
    proj = _norm_matmul(h2, nw, w_in_p, tn=w_in_p.shape[1] // 3)
    tile2 = lambda w: jnp.tile(w, LANES // dh).reshape(1, LANES)
    qn, qr, ks, kw, vst, vwt, kc, vc, gates = _nsa_prep(
        proj, pos.reshape(B * S, 1), inv, tile2(q_norm), tile2(k_norm[1]), tile2(k_norm[2]), bd, B, S)

    nch = S // CMP_STRIDE
    cw = CMP_STRIDE * dh
    kcc, vcct = _nsa_compress(
        kc.reshape(B, Hk, nch, cw), vc.reshape(B, Hk, nch, cw),
        cmp_pe.reshape(2, 2, cw), cmp_w1.reshape(2, 2, cw, CMP_HIDDEN).astype(BF16),
        cmp_b1.reshape(2, 1, CMP_HIDDEN), cmp_w2[0].astype(BF16), cmp_b2[0].reshape(1, dh),
        cmp_w2[1].T.astype(BF16), cmp_b2[1].reshape(dh, 1), k_norm[0].reshape(1, dh))

    o = _nsa_attn(qn, qr, kcc, vcct, ovlt, ks, vst, kw, vwt, gates, B, S)
    return _matmul_res(o, w_out.astype(BF16), h2)


def kernel(x, positions, ffn_norm, ffn_w_gate_up, ffn_w_down, mixer_norm, gdn_w_in, gdn_conv_w, gdn_A_log, gdn_dt_bias, gdn_out_norm, gdn_w_out, sc_w_in, sc_conv_w, sc_w_out, nsa_w_in, nsa_q_norm, nsa_k_norm, nsa_cmp_pe, nsa_cmp_w1, nsa_cmp_b1, nsa_cmp_w2, nsa_cmp_b2, nsa_w_out):
    B, S, D = x.shape
    depth = ffn_norm.shape[0]
    h = x.reshape(B * S, D)
    for layer in range(depth):
        h = _ffn(h, ffn_norm[layer, 0].reshape(1, D), *_prep_ffn_weights(ffn_w_gate_up[layer, 0], ffn_w_down[layer, 0]))
        nw = mixer_norm[layer].reshape(1, D)
        kind = layer % N_MIXERS
        j = layer // N_MIXERS
        if kind == 0:
            h = _gdn_mixer(h, nw, gdn_w_in[j], gdn_conv_w[j], gdn_A_log[j], gdn_dt_bias[j],
                           gdn_out_norm[j], gdn_w_out[j], B, S)
        elif kind == 1:
            h = _sc_mixer(h, nw, sc_w_in[j], sc_conv_w[j], sc_w_out[j], B, S)
        else:
            h = _nsa_mixer(h, nw, positions, nsa_w_in[j], nsa_q_norm[j], nsa_k_norm[j], nsa_cmp_pe[j],
                           nsa_cmp_w1[j], nsa_cmp_b1[j], nsa_cmp_w2[j], nsa_cmp_b2[j], nsa_w_out[j], B, S)
        h = _ffn(h, ffn_norm[layer, 1].reshape(1, D), *_prep_ffn_weights(ffn_w_gate_up[layer, 1], ffn_w_down[layer, 1]))
    return h.reshape(B, S, D)
```

```python
import functools

import numpy as np
import jax
import jax.numpy as jnp
from jax import lax
from jax.experimental import pallas as pl
from jax.experimental.pallas import tpu as pltpu

F32 = jnp.float32
BF16 = jnp.bfloat16

EPS = 1e-6
NEG = -1e30
FORCE = 1e9
S_FAR = 1 << 30
LANES = 128

GDN_HEADS = 8
GDN_DK = 128
GDN_DV = 128
GDN_CONV = 4
GDN_CHUNK = 64
SC_WIDTH = 3
NSA_HEADS = 16
NSA_KV_HEADS = 4
NSA_GROUP = NSA_HEADS // NSA_KV_HEADS
NSA_DH = 64
CMP_BLOCK = 32
CMP_STRIDE = 16
CMP_HIDDEN = 256
SLC_BLOCK = 64
SLC_TOPK = 16
N_LOCAL = 2
WINDOW = 512
ROPE_THETA = 10000.0
N_MIXERS = 3
GATE_ROWS = 16

VMEM_LIMIT = 56 * 1024 * 1024


def _cparams(sem):
    return pltpu.CompilerParams(dimension_semantics=sem, vmem_limit_bytes=VMEM_LIMIT)


def _resident(shape):
    nd = len(shape)
    return pl.BlockSpec(shape, lambda *_: (0,) * nd, pipeline_mode=pl.Buffered(1))


def _rms(x, w):
    return x * lax.rsqrt(jnp.mean(x * x, axis=-1, keepdims=True) + EPS) * w


def _silu(x):
    return x * jax.nn.sigmoid(x)


def _softplus(x):
    return jnp.maximum(x, 0.0) + jnp.log1p(jnp.exp(-jnp.abs(x)))


def _bf(x):
    return x.astype(BF16)


def _dot(a, b):
    return jnp.dot(a, b, preferred_element_type=F32)


def _dot_nt(a, b):
    return lax.dot_general(a, b, (((1,), (1,)), ((), ())), preferred_element_type=F32)


def _split(x):
    hi = x.astype(BF16)
    lo = (x - hi.astype(F32)).astype(BF16)
    return hi, lo


def _dot2_lhs(a, b_bf16):
    ah, al = _split(a)
    return _dot(ah, b_bf16) + _dot(al, b_bf16)


def _ffn_kernel(*refs, tf, with_proj):
    if with_proj:
        m_ref, wp_ref, x_ref, nw_ref, wg_ref, wu_ref, wd_ref, o_ref, a_ref = refs
        x = x_ref[...] + _dot(m_ref[...], wp_ref[...])
    else:
        x_ref, nw_ref, wg_ref, wu_ref, wd_ref, o_ref, a_ref = refs
        x = x_ref[...]
    xn = _bf(_rms(x, nw_ref[...]))
    for c in range(wg_ref.shape[1] // tf):
        cols = slice(c * tf, (c + 1) * tf)
        a_ref[:, cols] = _bf(_silu(_dot(xn, wg_ref[:, cols])) * _dot(xn, wu_ref[:, cols]))
    o_ref[...] = x + 0.5 * _dot(a_ref[...], wd_ref[...])


def _ffn(h2, nw, wg, wu, wd, proj=None, tm=512, tf=256):
    T, D = h2.shape
    row = lambda n: pl.BlockSpec((tm, n), lambda i: (i, 0))
    args, specs = [h2, nw, wg, wu, wd], [row(D)] + [_resident(a.shape) for a in (nw, wg, wu, wd)]
    if proj is not None:
        args = list(proj) + args
        specs = [row(proj[0].shape[1]), _resident(proj[1].shape)] + specs
    return pl.pallas_call(
        functools.partial(_ffn_kernel, tf=tf, with_proj=proj is not None),
        grid=(T // tm,),
        in_specs=specs,
        out_specs=row(D),
        out_shape=jax.ShapeDtypeStruct((T, D), F32),
        scratch_shapes=[pltpu.VMEM((tm, wg.shape[1]), BF16)],
        compiler_params=_cparams(("parallel",)),
        name="ffn",
    )(*args)


def _prep_ffn_weights(w_gate_up, w_down):
    fh = w_gate_up.shape[1] // 2
    return w_gate_up[:, :fh].astype(BF16), w_gate_up[:, fh:].astype(BF16), w_down.astype(BF16)


def _norm_matmul_kernel(x_ref, nw_ref, w_ref, o_ref, *, tn):
    xn = _rms(x_ref[...], nw_ref[...]).astype(BF16)
    for c in range(w_ref.shape[1] // tn):
        o_ref[:, c * tn:(c + 1) * tn] = _dot(xn, w_ref[:, c * tn:(c + 1) * tn])


def _norm_matmul(h2, nw, w, tn, tm=256):
    T, D = h2.shape
    N = w.shape[1]
    return pl.pallas_call(
        functools.partial(_norm_matmul_kernel, tn=tn),
        grid=(T // tm,),
        in_specs=[pl.BlockSpec((tm, D), lambda i: (i, 0)), _resident(nw.shape), _resident(w.shape)],
        out_specs=pl.BlockSpec((tm, N), lambda i: (i, 0)),
        out_shape=jax.ShapeDtypeStruct((T, N), F32),
        compiler_params=_cparams(("parallel",)),
        name="norm_matmul",
    )(h2, nw, w)


def _gdn_kernel(qkv_ref, gate_ref, ab_ref, cw_ref, alog_ref, dtb_ref, onw_ref, o_ref, xbuf_ref, s_ref, *, nch):
    C = GDN_CHUNK
    H, DK, DV = GDN_HEADS, GDN_DK, GDN_DV
    R = nch * C
    n = pl.program_id(1)

    @pl.when(n == 0)
    def _():
        xbuf_ref[0:8, :] = jnp.zeros((8, xbuf_ref.shape[1]), F32)
        s_ref[...] = jnp.zeros_like(s_ref)

    raw = qkv_ref[...]
    xbuf_ref[8:8 + R, :] = raw
    cw = cw_ref[...]
    y = (xbuf_ref[5:5 + R, :] * cw[0:1] + xbuf_ref[6:6 + R, :] * cw[1:2]
         + xbuf_ref[7:7 + R, :] * cw[2:3] + raw * cw[3:4])
    xbuf_ref[0:8, :] = raw[R - 8:R]
    y = _silu(y)

    ab = ab_ref[...]
    g_all = -jnp.exp(alog_ref[...]) * _softplus(ab + dtb_ref[...])
    beta_all = jax.nn.sigmoid(ab)
    rowc = lax.broadcasted_iota(jnp.int32, (R, LANES), 0) % C
    gc = g_all
    s = 1
    while s < C:
        gc = gc + jnp.where(rowc >= s, pltpu.roll(gc, s, axis=0), 0.0)
        s *= 2
    egc = jnp.exp(gc)
    gcc = [gc[c * C:(c + 1) * C] for c in range(nch)]
    gct = [g.T for g in gcc]
    glast = [g[C - 1:C, :] for g in gcc]
    ekd = [jnp.exp(glast[c] - gcc[c]) for c in range(nch)]
    egl = [jnp.exp(glast[c]) for c in range(nch)]

    ri = lax.broadcasted_iota(jnp.int32, (C, C), 0)
    ci = lax.broadcasted_iota(jnp.int32, (C, C), 1)
    lower = ri >= ci
    strict = ri > ci

    pairs = [(c, h) for c in range(nch) for h in range(H)]
    dec, qnb, knb, kbb, sol, qd, kdt = {}, {}, {}, {}, {}, {}, {}
    for p in pairs:
        c, h = p
        rows = slice(c * C, (c + 1) * C)
        gcol = gcc[c][:, h:h + 1]
        grow = gct[c][h:h + 1, :]
        beta = beta_all[rows, H + h:H + h + 1]
        e = egc[rows, h:h + 1]
        dec[p] = jnp.where(lower, jnp.exp(jnp.where(lower, gcol - grow, 0.0)), 0.0)
        q = y[rows, h * DK:(h + 1) * DK]
        k = y[rows, H * DK + h * DK:H * DK + (h + 1) * DK]
        v = y[rows, 2 * H * DK + h * DV:2 * H * DK + (h + 1) * DV]
        qn = q * lax.rsqrt(jnp.sum(q * q, -1, keepdims=True) + EPS) * (DK ** -0.5)
        kn = k * lax.rsqrt(jnp.sum(k * k, -1, keepdims=True) + EPS)
        kb = kn * beta
        qnb[p], knb[p], kbb[p] = _bf(qn), _bf(kn), _bf(kb)
        sol[p] = jnp.concatenate([v * beta, kb * e], axis=1)
        qd[p] = qn * e
        kdt[p] = _bf((kn * ekd[c][:, h:h + 1]).T)

    kk = {p: _dot_nt(kbb[p], knb[p]) for p in pairs}
    qk = {p: _dot_nt(qnb[p], knb[p]) for p in pairs}
    nil = {p: jnp.where(strict, -kk[p] * dec[p], 0.0) for p in pairs}
    attn = {p: _bf(jnp.where(lower, qk[p] * dec[p], 0.0)) for p in pairs}
    x = {p: _bf(nil[p]) for p in pairs}
    pw = 2
    while pw < C:
        x2 = {p: _dot(x[p], x[p]) for p in pairs}
        x = {p: _bf(x2[p]) for p in pairs}
        nil = {p: nil[p] + x2[p] + _dot(_bf(nil[p]), x[p]) for p in pairs}
        pw *= 2
    sol = {p: sol[p] + _dot(_bf(nil[p]), _bf(sol[p])) for p in pairs}

    st = [s_ref[h] for h in range(H)]
    for c in range(nch):
        r = [_dot(_bf(jnp.concatenate([sol[(c, h)][:, DV:], qd[(c, h)]], axis=0)), _bf(st[h])) for h in range(H)]
        vb = [_bf(sol[(c, h)][:, :DV] - r[h][:C]) for h in range(H)]
        o = [r[h][C:] + _dot(attn[(c, h)], vb[h]) for h in range(H)]
        st = [st[h] * egl[c][:, h:h + 1] + _dot(kdt[(c, h)], vb[h]) for h in range(H)]
        for h in range(H):
            gt = gate_ref[c * C:(c + 1) * C, h * DV:(h + 1) * DV]
            o_ref[c * C:(c + 1) * C, h * DV:(h + 1) * DV] = _bf(_rms(o[h], onw_ref[...]) * _silu(gt))
    for h in range(H):
        s_ref[h] = st[h]


def _gdn_core(proj, cw, alog, dtb, onw, B, S, nch=2):
    R = nch * GDN_CHUNK
    NQ = 2 * GDN_HEADS * GDN_DK + GDN_HEADS * GDN_DV
    NG = GDN_HEADS * GDN_DV
    nstep = S // R
    T = B * S
    return pl.pallas_call(
        functools.partial(_gdn_kernel, nch=nch),
        grid=(B, nstep),
        in_specs=[pl.BlockSpec((R, NQ), lambda b, n: (b * nstep + n, 0)),
                  pl.BlockSpec((R, NG), lambda b, n: (b * nstep + n, NQ // NG)),
                  pl.BlockSpec((R, LANES), lambda b, n: (b * nstep + n, (NQ + NG) // LANES)),
                  _resident(cw.shape), _resident(alog.shape), _resident(dtb.shape), _resident(onw.shape)],
        out_specs=pl.BlockSpec((R, NG), lambda b, n: (b * nstep + n, 0)),
        out_shape=jax.ShapeDtypeStruct((T, NG), BF16),
        scratch_shapes=[pltpu.VMEM((8 + R, NQ), F32), pltpu.VMEM((GDN_HEADS, GDN_DK, GDN_DV), F32)],
        compiler_params=_cparams(("parallel", "arbitrary")),
        name="gdn_core",
    )(proj, proj, proj, cw, alog, dtb, onw)


def _gdn_mixer(h2, nw, w_in, conv_w, a_log, dt_bias, out_norm, w_out, B, S):
    H = GDN_HEADS
    n_used = w_in.shape[1]
    n_pad = -n_used % LANES
    w_in_p = jnp.pad(w_in, ((0, 0), (0, n_pad))).astype(BF16)
    proj = _norm_matmul(h2, nw, w_in_p, tn=w_in_p.shape[1] // 3)
    lane_pad = LANES - H
    alog = jnp.pad(a_log, (0, lane_pad)).reshape(1, LANES)
    dtb = jnp.pad(dt_bias, (0, lane_pad)).reshape(1, LANES)
    o = _gdn_core(proj, conv_w, alog, dtb, out_norm.reshape(1, GDN_DV), B, S)
    return h2, (o, w_out.astype(BF16))


def _sc_kernel(h_ref, nw_ref, win_ref, cw_ref, wout_ref, o_ref, cx_ref):
    tm, D = h_ref.shape
    i = pl.program_id(1)

    @pl.when(i == 0)
    def _():
        cx_ref[0:8, :] = jnp.zeros((8, D), F32)

    x = h_ref[...]
    xn = _rms(x, nw_ref[...]).astype(BF16)
    bg = _dot(xn, win_ref[:, 0:D])
    cx = _dot(xn, win_ref[:, D:2 * D]) * _dot(xn, win_ref[:, 2 * D:3 * D])
    cx_ref[8:8 + tm, :] = cx
    cw = cw_ref[...]
    y = cx_ref[6:6 + tm, :] * cw[0:1] + cx_ref[7:7 + tm, :] * cw[1:2] + cx * cw[2:3]
    cx_ref[0:8, :] = cx[tm - 8:tm]
    o_ref[...] = x + _dot((bg * y).astype(BF16), wout_ref[...])


def _sc_mixer(h2, nw, w_in, conv_w, w_out, B, S, tm=256):
    T, D = h2.shape
    nt = S // tm
    return pl.pallas_call(
        _sc_kernel,
        grid=(B, nt),
        in_specs=[pl.BlockSpec((tm, D), lambda b, i: (b * nt + i, 0)),
                  _resident(nw.shape), _resident(w_in.shape), _resident(conv_w.shape), _resident(w_out.shape)],
        out_specs=pl.BlockSpec((tm, D), lambda b, i: (b * nt + i, 0)),
        out_shape=jax.ShapeDtypeStruct((T, D), F32),
        scratch_shapes=[pltpu.VMEM((8 + tm, D), F32)],
        compiler_params=_cparams(("parallel", "arbitrary")),
        name="short_conv",
    )(h2, nw, w_in.astype(BF16), conv_w, w_out.astype(BF16))


_NQ = NSA_HEADS * NSA_DH
_KVW = NSA_KV_HEADS * NSA_DH
_OFF_KC, _OFF_VC, _OFF_KS, _OFF_VS, _OFF_KW, _OFF_VW = (_NQ + i * _KVW for i in range(6))
_OFF_G = _NQ + 6 * _KVW
KT = 128
SEL_TK = 512
LOG2E = float(np.log2(np.e))


def _half(slab, j):
    if j:
        slab = pltpu.roll(slab, NSA_DH, axis=1)
    return slab[:, 0:NSA_DH]


def _nsa_prep_kernel(p_ref, pos_ref, inv_ref, qw_ref, ksw_ref, kww_ref, bd_ref,
                     qn_ref, qr_ref, ks_ref, kw_ref, vs_ref, vw_ref, kc_ref, vc_ref, g_ref):
    tm = p_ref.shape[0]
    dh = NSA_DH
    ang = pos_ref[...].astype(F32) * inv_ref[...]
    cos = jnp.cos(ang)
    sin = jnp.sin(ang)
    lane = lax.broadcasted_iota(jnp.int32, (tm, LANES), 1)
    first_half = (lane % dh) < (dh // 2)
    bd = bd_ref[...]

    def seg_rms(x, w):
        ss = _dot2_lhs(x * x, bd)
        return x * lax.rsqrt(ss * (1.0 / dh) + EPS) * w

    def rope(x):
        rot = jnp.where(first_half, -pltpu.roll(x, LANES - dh // 2, axis=1), pltpu.roll(x, dh // 2, axis=1))
        return x * cos + rot * sin

    def put(ref, slab, c):
        for j in range(2):
            ref[0, 2 * c + j] = _half(slab, j).astype(ref.dtype)

    def put_t(ref, slab, c):
        t = slab.T
        for j in range(2):
            ref[0, 2 * c + j] = t[j * dh:(j + 1) * dh].astype(ref.dtype)

    def put_t_tiles(ref, slab, c):
        t = slab.T
        for j in range(2):
            for tt in range(tm // KT):
                ref[0, 2 * c + j, tt] = t[j * dh:(j + 1) * dh, tt * KT:(tt + 1) * KT].astype(ref.dtype)

    def put_aug(ref, slab, c):
        row = lax.broadcasted_iota(jnp.int32, (tm, LANES), 0)
        onehot = jnp.where(lane - dh == (row // SLC_BLOCK) % (SEL_TK // SLC_BLOCK), 1.0, 0.0)
        for j in range(2):
            kj = pltpu.roll(slab, dh, axis=1) if j else slab
            ref[0, 2 * c + j] = jnp.where(lane < dh, kj, onehot).astype(ref.dtype)

    scale = dh ** -0.5 * LOG2E
    for c in range(_NQ // LANES):
        xn = seg_rms(p_ref[:, c * LANES:(c + 1) * LANES], qw_ref[...])
        put_t(qn_ref, xn * scale, c)
        put_t(qr_ref, rope(xn) * scale, c)
    for c in range(_KVW // LANES):
        put_aug(ks_ref, rope(seg_rms(p_ref[:, _OFF_KS + c * LANES:_OFF_KS + (c + 1) * LANES], ksw_ref[...])), c)
        put(kw_ref, rope(seg_rms(p_ref[:, _OFF_KW + c * LANES:_OFF_KW + (c + 1) * LANES], kww_ref[...])), c)
        put_t_tiles(vs_ref, p_ref[:, _OFF_VS + c * LANES:_OFF_VS + (c + 1) * LANES], c)
        put_t_tiles(vw_ref, p_ref[:, _OFF_VW + c * LANES:_OFF_VW + (c + 1) * LANES], c)
        put(kc_ref, p_ref[:, _OFF_KC + c * LANES:_OFF_KC + (c + 1) * LANES], c)
        put(vc_ref, p_ref[:, _OFF_VC + c * LANES:_OFF_VC + (c + 1) * LANES], c)
    g_ref[0] = jax.nn.sigmoid(p_ref[:, _OFF_G:_OFF_G + LANES]).T[0:g_ref.shape[1]]


def _nsa_prep(proj, pos, inv, qw, ksw, kww, bd, B, S, tm=SEL_TK):
    H, Hk, dh = NSA_HEADS, NSA_KV_HEADS, NSA_DH
    nt = S // tm
    NP = proj.shape[1]

    def pm(nh, dt):
        return (pl.BlockSpec((1, nh, tm, dh), lambda b, i: (b, 0, i, 0)),
                jax.ShapeDtypeStruct((B, nh, S, dh), dt))

    def dm(nh):
        return (pl.BlockSpec((1, nh, dh, tm), lambda b, i: (b, 0, 0, i)),
                jax.ShapeDtypeStruct((B, nh, dh, S), BF16))

    def dmt(nh):
        return (pl.BlockSpec((1, nh, tm // KT, dh, KT), lambda b, i: (b, 0, i, 0, 0)),
                jax.ShapeDtypeStruct((B, nh, S // KT, dh, KT), BF16))

    gr = Hk * GATE_ROWS
    aug = (pl.BlockSpec((1, Hk, tm, LANES), lambda b, i: (b, 0, i, 0)), jax.ShapeDtypeStruct((B, Hk, S, LANES), BF16))
    outs = [dm(H), dm(H), aug, pm(Hk, BF16), dmt(Hk), dmt(Hk), pm(Hk, F32), pm(Hk, F32),
            (pl.BlockSpec((1, gr, tm), lambda b, i: (b, 0, i)), jax.ShapeDtypeStruct((B, gr, S), F32))]
    return pl.pallas_call(
        _nsa_prep_kernel,
        grid=(B, nt),
        in_specs=[pl.BlockSpec((tm, NP), lambda b, i: (b * nt + i, 0)),
                  pl.BlockSpec((tm, 1), lambda b, i: (b * nt + i, 0)),
                  _resident(inv.shape), _resident(qw.shape), _resident(ksw.shape), _resident(kww.shape),
                  _resident(bd.shape)],
        out_specs=[o[0] for o in outs],
        out_shape=[o[1] for o in outs],
        compiler_params=_cparams(("parallel", "parallel")),
        name="nsa_prep",
    )(proj, pos, inv, qw, ksw, kww, bd)


def _gelu_tanh(x):
    return 0.5 * x * (1.0 + jnp.tanh(np.float32(np.sqrt(2.0 / np.pi)) * (x + 0.044715 * (x * x * x))))


def _cmp_kernel(kc_ref, vc_ref, pe_ref, w1_ref, b1_ref, w2k_ref, b2k_ref, w2vt_ref, b2v_ref, knw_ref, kco_ref, vco_ref):
    nch = kc_ref.shape[2]

    def hidden(src, i):
        x = src[0, 0]
        p1 = _dot(_bf(x + pe_ref[i, 0:1]), w1_ref[i, 0])
        p2 = _dot(_bf(x + pe_ref[i, 1:2]), w1_ref[i, 1])
        return _bf(_gelu_tanh(p1 + pltpu.roll(p2, nch - 1, axis=0) + b1_ref[i]))

    kcc = _dot(hidden(kc_ref, 0), w2k_ref[...]) + b2k_ref[...]
    kco_ref[0, 0] = _bf(_rms(kcc, knw_ref[...]))
    vco_ref[0, 0] = _bf(_dot_nt(w2vt_ref[...], hidden(vc_ref, 1)) + b2v_ref[...])


def _nsa_compress(kc, vc, pe, w1, b1, w2k, b2k, w2vt, b2v, knw):
    B, Hk, nch, cw = kc.shape
    dh = NSA_DH
    spec = pl.BlockSpec((1, 1, nch, cw), lambda b, h: (b, h, 0, 0))
    return pl.pallas_call(
        _cmp_kernel,
        grid=(B, Hk),
        in_specs=[spec, spec] + [_resident(a.shape) for a in (pe, w1, b1, w2k, b2k, w2vt, b2v, knw)],
        out_specs=[pl.BlockSpec((1, 1, nch, dh), lambda b, h: (b, h, 0, 0)),
                   pl.BlockSpec((1, 1, dh, nch), lambda b, h: (b, h, 0, 0))],
        out_shape=[jax.ShapeDtypeStruct((B, Hk, nch, dh), BF16), jax.ShapeDtypeStruct((B, Hk, dh, nch), BF16)],
        compiler_params=_cparams(("parallel", "parallel")),
        name="nsa_compress",
    )(kc, vc, pe, w1, b1, w2k, b2k, w2vt, b2v, knw)


def _nsa_attn_kernel(qn_ref, qr_ref, kcc_ref, vcct_ref, ovlt_ref, ks_ref, vst_ref, kw_ref, vwt_ref, g_ref, o_ref,
                     selb_ref, *, nsel):
    G, dh, tq = qn_ref.shape[2:]
    nck = kcc_ref.shape[2]
    ns = ovlt_ref.shape[0]
    tk = SEL_TK
    assert tk // SLC_BLOCK == 8
    s0 = pl.program_id(2) * tq
    tpos = s0 + lax.broadcasted_iota(jnp.int32, (1, tq), 1)
    heads = range(G)

    kcc = kcc_ref[0, 0]
    cend = lax.broadcasted_iota(jnp.int32, (nck, 1), 0) * CMP_STRIDE + (CMP_BLOCK - 1)
    cbias = jnp.where(cend <= tpos, 0.0, NEG)
    has_c = jnp.where(tpos >= CMP_BLOCK - 1, 1.0, 0.0)
    sc = [_dot(kcc, qn_ref[0, 0, g]) + cbias for g in heads]
    ec = [jnp.exp2(sc[g] - jnp.max(sc[g], axis=0, keepdims=True)) for g in heads]
    pc = [ec[g] * (has_c / jnp.sum(ec[g], axis=0, keepdims=True)) for g in heads]
    oc = [_dot(vcct_ref[0, 0], _bf(pc[g])) for g in heads]
    psum = pc[0]
    for g in range(1, G):
        psum = psum + pc[g]
    imp = _dot(ovlt_ref[...], _bf(psum))

    blk = lax.broadcasted_iota(jnp.int32, (ns, tq), 0)
    svalid = blk * SLC_BLOCK <= tpos
    dist = tpos // SLC_BLOCK - blk
    forced = (blk == 0) | ((dist >= 0) & (dist < N_LOCAL))
    score = jnp.where(svalid & forced, FORCE, jnp.where(svalid, imp, -1.0))
    ngrp = ns // 8
    selb = [[] for _ in range(ngrp)]
    for w in range(tq // LANES):
        sw_ = score[:, w * LANES:(w + 1) * LANES]
        sg = [sw_[8 * r:8 * r + 8] for r in range(ngrp)]
        bg = [blk[8 * r:8 * r + 8, 0:LANES] for r in range(ngrp)]
        rank = [jnp.zeros((8, LANES), F32) for _ in range(ngrp)]
        for j in range(ns):
            rowj = sw_[j:j + 1, :]
            for r in range(ngrp):
                if 8 * r + 7 <= j:
                    beat = jnp.where(rowj > sg[r], 1.0, 0.0)
                elif 8 * r > j:
                    beat = jnp.where(sg[r] > rowj, 0.0, 1.0)
                else:
                    beat = jnp.where(bg[r] > j, jnp.where(sg[r] > rowj, 0.0, 1.0),
                                     jnp.where(rowj > sg[r], 1.0, 0.0))
                rank[r] = rank[r] + beat
        for r in range(ngrp):
            selb[r].append(jnp.where((rank[r] < nsel) & (sg[r] >= 0.0), 0.0, NEG))
    zpad = jnp.zeros((8, tq), F32)
    for t in range(ngrp):
        selb_ref[t] = _bf(jnp.concatenate([jnp.concatenate(selb[t], axis=1), zpad], axis=0))

    nsub = tk // KT
    qr = [qr_ref[0, 0, g] for g in heads]
    qpad = jnp.zeros((LANES - dh - 16, tq), BF16)

    def tile_scores(kt):
        kaug = ks_ref[0, 0, pl.ds(pl.multiple_of(kt * tk, tk), tk), :]
        sb = selb_ref[kt]
        return tuple(_dot(kaug, jnp.concatenate([qr[g], sb, qpad], axis=0)) for g in heads)

    def update(stats, s, kt):
        m, l, acc = stats
        vt = jnp.concatenate([vst_ref[0, 0, kt * nsub + j] for j in range(nsub)], axis=1)
        mn = [jnp.maximum(m[g], jnp.max(s[g], axis=0, keepdims=True)) for g in heads]
        p = [jnp.exp2(s[g] - mn[g]) for g in heads]
        alpha = [jnp.exp2(m[g] - mn[g]) for g in heads]
        l = [alpha[g] * l[g] + jnp.sum(p[g], axis=0, keepdims=True) for g in heads]
        acc = [alpha[g] * acc[g] + _dot(vt, _bf(p[g])) for g in heads]
        return tuple(mn), tuple(l), tuple(acc)

    def body(kt, stats):
        return update(stats, tile_scores(kt), kt)

    init = (tuple(jnp.full((1, tq), NEG, F32) for _ in heads),
            tuple(jnp.zeros((1, tq), F32) for _ in heads),
            tuple(jnp.zeros((dh, tq), F32) for _ in heads))
    last = (s0 + tq - 1) // tk
    stats = lax.fori_loop(0, last, body, init)
    s = tile_scores(last)
    kpos = last * tk + lax.broadcasted_iota(jnp.int32, (tk, 1), 0)
    causal = jnp.where(kpos <= tpos, 0.0, NEG)
    _, l, acc = update(stats, [s[g] + causal for g in heads], last)
    osel = [acc[g] * (1.0 / l[g]) for g in heads]

    ksub = lax.broadcasted_iota(jnp.int32, (KT, 1), 0)
    sw = [[] for _ in heads]
    vts = []
    for j in range((WINDOW + tq) // KT):
        start = s0 - WINDOW + j * KT
        st = pl.multiple_of(jnp.maximum(start, 0), KT)
        kpos = jnp.where(start >= 0, st, S_FAR) + ksub
        wbias = jnp.where((kpos <= tpos) & (kpos > tpos - WINDOW), 0.0, NEG)
        kj = kw_ref[0, 0, pl.ds(st, KT), :]
        vts.append(vwt_ref[0, 0, st // KT])
        for g in heads:
            sw[g].append(_dot(kj, qr[g]) + wbias)
    ow = []
    for g in heads:
        mw = sw[g][0].max(axis=0, keepdims=True)
        for j in range(1, len(vts)):
            mw = jnp.maximum(mw, sw[g][j].max(axis=0, keepdims=True))
        ew = [jnp.exp2(sw[g][j] - mw) for j in range(len(vts))]
        lw = ew[0].sum(axis=0, keepdims=True)
        o = _dot(vts[0], _bf(ew[0]))
        for j in range(1, len(vts)):
            lw = lw + ew[j].sum(axis=0, keepdims=True)
            o = o + _dot(vts[j], _bf(ew[j]))
        ow.append(o * (1.0 / lw))

    gt = g_ref[0]
    outs = [gt[3 * g:3 * g + 1] * oc[g] + gt[3 * g + 1:3 * g + 2] * osel[g] + gt[3 * g + 2:3 * g + 3] * ow[g]
            for g in heads]
    for c in range(G // 2):
        o_ref[:, c * LANES:(c + 1) * LANES] = _bf(jnp.concatenate([outs[2 * c], outs[2 * c + 1]], axis=0).T)


def _nsa_attn(qn, qr, kcc, vcct, ovlt, ks, vst, kw, vwt, gates, B, S, tq=256):
    Hk, G, dh = NSA_KV_HEADS, NSA_GROUP, NSA_DH
    nt = S // tq
    nck = kcc.shape[2]
    ns = ovlt.shape[0]
    q5 = lambda q: q.reshape(B, Hk, G, dh, S)
    qspec = pl.BlockSpec((1, 1, G, dh, tq), lambda b, h, i: (b, h, 0, 0, i))
    vspec = pl.BlockSpec((1, 1, S // KT, dh, KT), lambda b, h, i: (b, h, 0, 0, 0))
    return pl.pallas_call(
        functools.partial(_nsa_attn_kernel, nsel=min(SLC_TOPK, ns)),
        grid=(B, Hk, nt),
        in_specs=[qspec, qspec,
                  pl.BlockSpec((1, 1, nck, dh), lambda b, h, i: (b, h, 0, 0)),
                  pl.BlockSpec((1, 1, dh, nck), lambda b, h, i: (b, h, 0, 0)),
                  _resident(ovlt.shape),
                  pl.BlockSpec((1, 1, S, LANES), lambda b, h, i: (b, h, 0, 0)), vspec,
                  pl.BlockSpec((1, 1, S, dh), lambda b, h, i: (b, h, 0, 0)), vspec,
                  pl.BlockSpec((1, GATE_ROWS, tq), lambda b, h, i: (b, h, i))],
        out_specs=pl.BlockSpec((tq, G * dh), lambda b, h, i: (b * nt + i, h)),
        out_shape=jax.ShapeDtypeStruct((B * S, Hk * G * dh), BF16),
        scratch_shapes=[pltpu.VMEM((S // SEL_TK, 16, tq), BF16)],
        compiler_params=_cparams(("parallel", "parallel", "parallel")),
        name="nsa_attn",
    )(q5(qn), q5(qr), kcc, vcct, ovlt, ks, vst, kw, vwt, gates)


def _nsa_constants(S):
    dh = NSA_DH
    half = dh // 2
    inv = 1.0 / (ROPE_THETA ** (np.arange(0, dh, 2, dtype=np.float32) / dh))
    inv = np.tile(inv.astype(np.float32), LANES // half).reshape(1, LANES)
    lane = np.arange(LANES)
    bd = (lane[:, None] // dh == lane[None, :] // dh).astype(np.float32)
    nck = S // CMP_STRIDE
    ns = S // SLC_BLOCK
    ci = np.arange(nck)[None, :]
    sj = np.arange(ns)[:, None]
    ovlt = np.clip(np.minimum(ci * CMP_STRIDE + CMP_BLOCK, (sj + 1) * SLC_BLOCK)
                   - np.maximum(ci * CMP_STRIDE, sj * SLC_BLOCK), 0, None).astype(np.float32) / CMP_BLOCK
    return jnp.asarray(inv), jnp.asarray(bd, BF16), jnp.asarray(ovlt, BF16)


def _nsa_mixer(h2, nw, pos, w_in, q_norm, k_norm, cmp_pe, cmp_w1, cmp_b1, cmp_w2, cmp_b2, w_out, B, S):
    H, Hk, G, dh = NSA_HEADS, NSA_KV_HEADS, NSA_GROUP, NSA_DH
    inv, bd, ovlt = _nsa_constants(S)
    D = w_in.shape[0]
    wg = jnp.pad(w_in[:, _OFF_G:].reshape(D, Hk, 3 * G), ((0, 0), (0, 0), (0, GATE_ROWS - 3 * G)))
    wg = jnp.pad(wg.reshape(D, Hk * GATE_ROWS), ((0, 0), (0, LANES - Hk * GATE_ROWS)))
    w_in_p = jnp.concatenate([w_in[:, :_OFF_G], wg], axis=1).astype(BF16)
    proj = _norm_matmul(h2, nw, w_in_p, tn=w_in_p.shape[1] // 3)
    tile2 = lambda w: jnp.tile(w, LANES // dh).reshape(1, LANES)
    qn, qr, ks, kw, vst, vwt, kc, vc, gates = _nsa_prep(
        proj, pos.reshape(B * S, 1), inv, tile2(q_norm), tile2(k_norm[1]), tile2(k_norm[2]), bd, B, S)

    nch = S // CMP_STRIDE
    cw = CMP_STRIDE * dh
    kcc, vcct = _nsa_compress(
        kc.reshape(B, Hk, nch, cw), vc.reshape(B, Hk, nch, cw),
        cmp_pe.reshape(2, 2, cw), cmp_w1.reshape(2, 2, cw, CMP_HIDDEN).astype(BF16),
        cmp_b1.reshape(2, 1, CMP_HIDDEN), cmp_w2[0].astype(BF16), cmp_b2[0].reshape(1, dh),
        cmp_w2[1].T.astype(BF16), cmp_b2[1].reshape(dh, 1), k_norm[0].reshape(1, dh))

    return h2, (_nsa_attn(qn, qr, kcc, vcct, ovlt, ks, vst, kw, vwt, gates, B, S), w_out.astype(BF16))


def kernel(x, positions, ffn_norm, ffn_w_gate_up, ffn_w_down, mixer_norm, gdn_w_in, gdn_conv_w, gdn_A_log, gdn_dt_bias, gdn_out_norm, gdn_w_out, sc_w_in, sc_conv_w, sc_w_out, nsa_w_in, nsa_q_norm, nsa_k_norm, nsa_cmp_pe, nsa_cmp_w1, nsa_cmp_b1, nsa_cmp_w2, nsa_cmp_b2, nsa_w_out):
    B, S, D = x.shape
    depth = ffn_norm.shape[0]
    h = x.reshape(B * S, D)
    for layer in range(depth):
        h = _ffn(h, ffn_norm[layer, 0].reshape(1, D), *_prep_ffn_weights(ffn_w_gate_up[layer, 0], ffn_w_down[layer, 0]))
        nw = mixer_norm[layer].reshape(1, D)
        kind = layer % N_MIXERS
        j = layer // N_MIXERS
        proj = None
        if kind == 0:
            h, proj = _gdn_mixer(h, nw, gdn_w_in[j], gdn_conv_w[j], gdn_A_log[j], gdn_dt_bias[j],
                                 gdn_out_norm[j], gdn_w_out[j], B, S)
        elif kind == 1:
            h = _sc_mixer(h, nw, sc_w_in[j], sc_conv_w[j], sc_w_out[j], B, S)
        else:
            h, proj = _nsa_mixer(h, nw, positions, nsa_w_in[j], nsa_q_norm[j], nsa_k_norm[j], nsa_cmp_pe[j],
                                 nsa_cmp_w1[j], nsa_cmp_b1[j], nsa_cmp_w2[j], nsa_cmp_b2[j], nsa_w_out[j], B, S)
        h = _ffn(h, ffn_norm[layer, 1].reshape(1, D),
                 *_prep_ffn_weights(ffn_w_gate_up[layer, 1], ffn_w_down[layer, 1]), proj=proj)
    return h.reshape(B, S, D)
```

```python
import functools

import numpy as np
import jax
import jax.numpy as jnp
from jax import lax
from jax.experimental import pallas as pl
from jax.experimental.pallas import tpu as pltpu

F32 = jnp.float32
BF16 = jnp.bfloat16

EPS = 1e-6
NEG = -1e30
FORCE = 1e9
S_FAR = 1 << 30
LANES = 128

GDN_HEADS = 8
GDN_DK = 128
GDN_DV = 128
GDN_CONV = 4
GDN_CHUNK = 64
SC_WIDTH = 3
NSA_HEADS = 16
NSA_KV_HEADS = 4
NSA_GROUP = NSA_HEADS // NSA_KV_HEADS
NSA_DH = 64
CMP_BLOCK = 32
CMP_STRIDE = 16
CMP_HIDDEN = 256
SLC_BLOCK = 64
SLC_TOPK = 16
N_LOCAL = 2
WINDOW = 512
ROPE_THETA = 10000.0
N_MIXERS = 3
GATE_ROWS = 16

VMEM_LIMIT = 56 * 1024 * 1024


def _cparams(sem):
    return pltpu.CompilerParams(dimension_semantics=sem, vmem_limit_bytes=VMEM_LIMIT)


def _resident(shape):
    nd = len(shape)
    return pl.BlockSpec(shape, lambda *_: (0,) * nd, pipeline_mode=pl.Buffered(1))


def _rms(x, w):
    return x * lax.rsqrt(jnp.mean(x * x, axis=-1, keepdims=True) + EPS) * w


def _silu(x):
    return x * jax.nn.sigmoid(x)


def _softplus(x):
    return jnp.maximum(x, 0.0) + jnp.log1p(jnp.exp(-jnp.abs(x)))


def _bf(x):
    return x.astype(BF16)


def _dot(a, b):
    return jnp.dot(a, b, preferred_element_type=F32)


def _dot_nt(a, b):
    return lax.dot_general(a, b, (((1,), (1,)), ((), ())), preferred_element_type=F32)


def _split(x):
    hi = x.astype(BF16)
    lo = (x - hi.astype(F32)).astype(BF16)
    return hi, lo


def _dot2_lhs(a, b_bf16):
    ah, al = _split(a)
    return _dot(ah, b_bf16) + _dot(al, b_bf16)


def _ffn_kernel(*refs, tf, with_proj):
    if with_proj:
        m_ref, wp_ref, x_ref, nw_ref, wg_ref, wu_ref, wd_ref, o_ref, a_ref = refs
        x = x_ref[...] + _dot(m_ref[...], wp_ref[...])
    else:
        x_ref, nw_ref, wg_ref, wu_ref, wd_ref, o_ref, a_ref = refs
        x = x_ref[...]
    xn = _bf(_rms(x, nw_ref[...]))
    for c in range(wg_ref.shape[1] // tf):
        cols = slice(c * tf, (c + 1) * tf)
        a_ref[:, cols] = _bf(_silu(_dot(xn, wg_ref[:, cols])) * _dot(xn, wu_ref[:, cols]))
    o_ref[...] = x + 0.5 * _dot(a_ref[...], wd_ref[...])


def _ffn(h2, nw, wg, wu, wd, proj=None, tm=512, tf=256):
    T, D = h2.shape
    row = lambda n: pl.BlockSpec((tm, n), lambda i: (i, 0))
    args, specs = [h2, nw, wg, wu, wd], [row(D)] + [_resident(a.shape) for a in (nw, wg, wu, wd)]
    if proj is not None:
        args = list(proj) + args
        specs = [row(proj[0].shape[1]), _resident(proj[1].shape)] + specs
    return pl.pallas_call(
        functools.partial(_ffn_kernel, tf=tf, with_proj=proj is not None),
        grid=(T // tm,),
        in_specs=specs,
        out_specs=row(D),
        out_shape=jax.ShapeDtypeStruct((T, D), F32),
        scratch_shapes=[pltpu.VMEM((tm, wg.shape[1]), BF16)],
        compiler_params=_cparams(("parallel",)),
        name="ffn",
    )(*args)


def _prep_ffn_weights(w_gate_up, w_down):
    fh = w_gate_up.shape[1] // 2
    return w_gate_up[:, :fh].astype(BF16), w_gate_up[:, fh:].astype(BF16), w_down.astype(BF16)


def _norm_matmul_kernel(x_ref, nw_ref, w_ref, o_ref, *, tn):
    xn = _rms(x_ref[...], nw_ref[...]).astype(BF16)
    for c in range(w_ref.shape[1] // tn):
        o_ref[:, c * tn:(c + 1) * tn] = _dot(xn, w_ref[:, c * tn:(c + 1) * tn])


def _norm_matmul(h2, nw, w, tn, tm=512):
    T, D = h2.shape
    N = w.shape[1]
    return pl.pallas_call(
        functools.partial(_norm_matmul_kernel, tn=tn),
        grid=(T // tm,),
        in_specs=[pl.BlockSpec((tm, D), lambda i: (i, 0)), _resident(nw.shape), _resident(w.shape)],
        out_specs=pl.BlockSpec((tm, N), lambda i: (i, 0)),
        out_shape=jax.ShapeDtypeStruct((T, N), F32),
        compiler_params=_cparams(("parallel",)),
        name="norm_matmul",
    )(h2, nw, w)


def _gdn_kernel(qkv_ref, gate_ref, ab_ref, cw_ref, alog_ref, dtb_ref, onw_ref, o_ref, xbuf_ref, s_ref, *, nch):
    C = GDN_CHUNK
    H, DK, DV = GDN_HEADS, GDN_DK, GDN_DV
    R = nch * C
    n = pl.program_id(1)

    @pl.when(n == 0)
    def _():
        xbuf_ref[0:8, :] = jnp.zeros((8, xbuf_ref.shape[1]), F32)
        s_ref[...] = jnp.zeros_like(s_ref)

    raw = qkv_ref[...]
    xbuf_ref[8:8 + R, :] = raw
    cw = cw_ref[...]
    y = (xbuf_ref[5:5 + R, :] * cw[0:1] + xbuf_ref[6:6 + R, :] * cw[1:2]
         + xbuf_ref[7:7 + R, :] * cw[2:3] + raw * cw[3:4])
    xbuf_ref[0:8, :] = raw[R - 8:R]
    y = _silu(y)

    ab = ab_ref[...]
    g_all = -jnp.exp(alog_ref[...]) * _softplus(ab + dtb_ref[...])
    beta_all = jax.nn.sigmoid(ab)
    rowc = lax.broadcasted_iota(jnp.int32, (R, LANES), 0) % C
    gc = g_all
    s = 1
    while s < C:
        gc = gc + jnp.where(rowc >= s, pltpu.roll(gc, s, axis=0), 0.0)
        s *= 2
    egc = jnp.exp(gc)
    gcc = [gc[c * C:(c + 1) * C] for c in range(nch)]
    gct = [g.T for g in gcc]
    glast = [g[C - 1:C, :] for g in gcc]
    ekd = [jnp.exp(glast[c] - gcc[c]) for c in range(nch)]
    egl = [jnp.exp(glast[c]) for c in range(nch)]

    ri = lax.broadcasted_iota(jnp.int32, (C, C), 0)
    ci = lax.broadcasted_iota(jnp.int32, (C, C), 1)
    lower = ri >= ci
    strict = ri > ci

    pairs = [(c, h) for c in range(nch) for h in range(H)]
    dec, qnb, knb, kbb, sol, qd, kdt = {}, {}, {}, {}, {}, {}, {}
    for p in pairs:
        c, h = p
        rows = slice(c * C, (c + 1) * C)
        gcol = gcc[c][:, h:h + 1]
        grow = gct[c][h:h + 1, :]
        beta = beta_all[rows, H + h:H + h + 1]
        e = egc[rows, h:h + 1]
        dec[p] = jnp.where(lower, jnp.exp(jnp.where(lower, gcol - grow, 0.0)), 0.0)
        q = y[rows, h * DK:(h + 1) * DK]
        k = y[rows, H * DK + h * DK:H * DK + (h + 1) * DK]
        v = y[rows, 2 * H * DK + h * DV:2 * H * DK + (h + 1) * DV]
        qn = q * lax.rsqrt(jnp.sum(q * q, -1, keepdims=True) + EPS) * (DK ** -0.5)
        kn = k * lax.rsqrt(jnp.sum(k * k, -1, keepdims=True) + EPS)
        kb = kn * beta
        qnb[p], knb[p], kbb[p] = _bf(qn), _bf(kn), _bf(kb)
        sol[p] = jnp.concatenate([v * beta, kb * e], axis=1)
        qd[p] = qn * e
        kdt[p] = _bf((kn * ekd[c][:, h:h + 1]).T)

    kk = {p: _dot_nt(kbb[p], knb[p]) for p in pairs}
    qk = {p: _dot_nt(qnb[p], knb[p]) for p in pairs}
    nil = {p: jnp.where(strict, -kk[p] * dec[p], 0.0) for p in pairs}
    attn = {p: _bf(jnp.where(lower, qk[p] * dec[p], 0.0)) for p in pairs}
    x = {p: _bf(nil[p]) for p in pairs}
    pw = 2
    while pw < C:
        x2 = {p: _dot(x[p], x[p]) for p in pairs}
        x = {p: _bf(x2[p]) for p in pairs}
        nil = {p: nil[p] + x2[p] + _dot(_bf(nil[p]), x[p]) for p in pairs}
        pw *= 2
    sol = {p: sol[p] + _dot(_bf(nil[p]), _bf(sol[p])) for p in pairs}

    st = [s_ref[h] for h in range(H)]
    for c in range(nch):
        r = [_dot(_bf(jnp.concatenate([sol[(c, h)][:, DV:], qd[(c, h)]], axis=0)), _bf(st[h])) for h in range(H)]
        vb = [_bf(sol[(c, h)][:, :DV] - r[h][:C]) for h in range(H)]
        o = [r[h][C:] + _dot(attn[(c, h)], vb[h]) for h in range(H)]
        st = [st[h] * egl[c][:, h:h + 1] + _dot(kdt[(c, h)], vb[h]) for h in range(H)]
        for h in range(H):
            gt = gate_ref[c * C:(c + 1) * C, h * DV:(h + 1) * DV]
            o_ref[c * C:(c + 1) * C, h * DV:(h + 1) * DV] = _bf(_rms(o[h], onw_ref[...]) * _silu(gt))
    for h in range(H):
        s_ref[h] = st[h]


def _gdn_core(proj, cw, alog, dtb, onw, B, S, nch=2):
    R = nch * GDN_CHUNK
    NQ = 2 * GDN_HEADS * GDN_DK + GDN_HEADS * GDN_DV
    NG = GDN_HEADS * GDN_DV
    nstep = S // R
    T = B * S
    return pl.pallas_call(
        functools.partial(_gdn_kernel, nch=nch),
        grid=(B, nstep),
        in_specs=[pl.BlockSpec((R, NQ), lambda b, n: (b * nstep + n, 0)),
                  pl.BlockSpec((R, NG), lambda b, n: (b * nstep + n, NQ // NG)),
                  pl.BlockSpec((R, LANES), lambda b, n: (b * nstep + n, (NQ + NG) // LANES)),
                  _resident(cw.shape), _resident(alog.shape), _resident(dtb.shape), _resident(onw.shape)],
        out_specs=pl.BlockSpec((R, NG), lambda b, n: (b * nstep + n, 0)),
        out_shape=jax.ShapeDtypeStruct((T, NG), BF16),
        scratch_shapes=[pltpu.VMEM((8 + R, NQ), F32), pltpu.VMEM((GDN_HEADS, GDN_DK, GDN_DV), F32)],
        compiler_params=_cparams(("parallel", "arbitrary")),
        name="gdn_core",
    )(proj, proj, proj, cw, alog, dtb, onw)


def _gdn_mixer(h2, nw, w_in, conv_w, a_log, dt_bias, out_norm, w_out, B, S):
    H = GDN_HEADS
    n_used = w_in.shape[1]
    n_pad = -n_used % LANES
    w_in_p = jnp.pad(w_in, ((0, 0), (0, n_pad))).astype(BF16)
    proj = _norm_matmul(h2, nw, w_in_p, tn=w_in_p.shape[1] // 3)
    lane_pad = LANES - H
    alog = jnp.pad(a_log, (0, lane_pad)).reshape(1, LANES)
    dtb = jnp.pad(dt_bias, (0, lane_pad)).reshape(1, LANES)
    o = _gdn_core(proj, conv_w, alog, dtb, out_norm.reshape(1, GDN_DV), B, S)
    return h2, (o, w_out.astype(BF16))


def _sc_kernel(h_ref, nw_ref, win_ref, cw_ref, wout_ref, o_ref, cx_ref):
    tm, D = h_ref.shape
    i = pl.program_id(1)

    @pl.when(i == 0)
    def _():
        cx_ref[0:8, :] = jnp.zeros((8, D), F32)

    x = h_ref[...]
    xn = _rms(x, nw_ref[...]).astype(BF16)
    bg = _dot(xn, win_ref[:, 0:D])
    cx = _dot(xn, win_ref[:, D:2 * D]) * _dot(xn, win_ref[:, 2 * D:3 * D])
    cx_ref[8:8 + tm, :] = cx
    cw = cw_ref[...]
    y = cx_ref[6:6 + tm, :] * cw[0:1] + cx_ref[7:7 + tm, :] * cw[1:2] + cx * cw[2:3]
    cx_ref[0:8, :] = cx[tm - 8:tm]
    o_ref[...] = x + _dot((bg * y).astype(BF16), wout_ref[...])


def _sc_mixer(h2, nw, w_in, conv_w, w_out, B, S, tm=512):
    T, D = h2.shape
    nt = S // tm
    return pl.pallas_call(
        _sc_kernel,
        grid=(B, nt),
        in_specs=[pl.BlockSpec((tm, D), lambda b, i: (b * nt + i, 0)),
                  _resident(nw.shape), _resident(w_in.shape), _resident(conv_w.shape), _resident(w_out.shape)],
        out_specs=pl.BlockSpec((tm, D), lambda b, i: (b * nt + i, 0)),
        out_shape=jax.ShapeDtypeStruct((T, D), F32),
        scratch_shapes=[pltpu.VMEM((8 + tm, D), F32)],
        compiler_params=_cparams(("parallel", "arbitrary")),
        name="short_conv",
    )(h2, nw, w_in.astype(BF16), conv_w, w_out.astype(BF16))


_NQ = NSA_HEADS * NSA_DH
_KVW = NSA_KV_HEADS * NSA_DH
_OFF_KC, _OFF_VC, _OFF_KS, _OFF_VS, _OFF_KW, _OFF_VW = (_NQ + i * _KVW for i in range(6))
_OFF_G = _NQ + 6 * _KVW
KT = 128
VR = NSA_DH + 16
SEL_TK = 512
LOG2E = float(np.log2(np.e))


def _half(slab, j):
    if j:
        slab = pltpu.roll(slab, NSA_DH, axis=1)
    return slab[:, 0:NSA_DH]


def _nsa_prep_kernel(p_ref, pos_ref, inv_ref, qw_ref, ksw_ref, kww_ref, bd_ref,
                     qn_ref, qr_ref, ks_ref, kw_ref, vs_ref, vw_ref, kc_ref, vc_ref, g_ref):
    tm = p_ref.shape[0]
    dh = NSA_DH
    ang = pos_ref[...].astype(F32) * inv_ref[...]
    cos = jnp.cos(ang)
    sin = jnp.sin(ang)
    lane = lax.broadcasted_iota(jnp.int32, (tm, LANES), 1)
    first_half = (lane % dh) < (dh // 2)
    bd = bd_ref[...]

    def seg_rms(x, w):
        ss = _dot2_lhs(x * x, bd)
        return x * lax.rsqrt(ss * (1.0 / dh) + EPS) * w

    def rope(x):
        rot = jnp.where(first_half, -pltpu.roll(x, LANES - dh // 2, axis=1), pltpu.roll(x, dh // 2, axis=1))
        return x * cos + rot * sin

    def put(ref, slab, c):
        for j in range(2):
            ref[0, 2 * c + j] = _half(slab, j).astype(ref.dtype)

    def put_t(ref, slab, c):
        t = slab.T
        for j in range(2):
            ref[0, 2 * c + j] = t[j * dh:(j + 1) * dh].astype(ref.dtype)

    def put_t_tiles(ref, slab, c):
        t = slab.T
        tail = jnp.where(lax.broadcasted_iota(jnp.int32, (VR - dh, tm), 0) == 0, 1.0, 0.0)
        for j in range(2):
            ext = jnp.concatenate([t[j * dh:(j + 1) * dh], tail], axis=0).astype(ref.dtype)
            for tt in range(tm // KT):
                ref[0, 2 * c + j, tt] = ext[:, tt * KT:(tt + 1) * KT]

    def put_aug(ref, slab, c):
        row = lax.broadcasted_iota(jnp.int32, (tm, LANES), 0)
        onehot = jnp.where(lane - dh == (row // SLC_BLOCK) % (SEL_TK // SLC_BLOCK), 1.0, 0.0)
        for j in range(2):
            kj = pltpu.roll(slab, dh, axis=1) if j else slab
            ref[0, 2 * c + j] = jnp.where(lane < dh, kj, onehot).astype(ref.dtype)

    scale = dh ** -0.5 * LOG2E
    for c in range(_NQ // LANES):
        xn = seg_rms(p_ref[:, c * LANES:(c + 1) * LANES], qw_ref[...])
        put_t(qn_ref, xn * scale, c)
        put_t(qr_ref, rope(xn) * scale, c)
    for c in range(_KVW // LANES):
        put_aug(ks_ref, rope(seg_rms(p_ref[:, _OFF_KS + c * LANES:_OFF_KS + (c + 1) * LANES], ksw_ref[...])), c)
        put(kw_ref, rope(seg_rms(p_ref[:, _OFF_KW + c * LANES:_OFF_KW + (c + 1) * LANES], kww_ref[...])), c)
        put_t_tiles(vs_ref, p_ref[:, _OFF_VS + c * LANES:_OFF_VS + (c + 1) * LANES], c)
        put_t_tiles(vw_ref, p_ref[:, _OFF_VW + c * LANES:_OFF_VW + (c + 1) * LANES], c)
        put(kc_ref, p_ref[:, _OFF_KC + c * LANES:_OFF_KC + (c + 1) * LANES], c)
        put(vc_ref, p_ref[:, _OFF_VC + c * LANES:_OFF_VC + (c + 1) * LANES], c)
    g_ref[0] = jax.nn.sigmoid(p_ref[:, _OFF_G:_OFF_G + LANES]).T[0:g_ref.shape[1]]


def _nsa_prep(proj, pos, inv, qw, ksw, kww, bd, B, S, tm=SEL_TK):
    H, Hk, dh = NSA_HEADS, NSA_KV_HEADS, NSA_DH
    nt = S // tm
    NP = proj.shape[1]

    def pm(nh, dt):
        return (pl.BlockSpec((1, nh, tm, dh), lambda b, i: (b, 0, i, 0)),
                jax.ShapeDtypeStruct((B, nh, S, dh), dt))

    def dm(nh):
        return (pl.BlockSpec((1, nh, dh, tm), lambda b, i: (b, 0, 0, i)),
                jax.ShapeDtypeStruct((B, nh, dh, S), BF16))

    def dmt(nh):
        return (pl.BlockSpec((1, nh, tm // KT, VR, KT), lambda b, i: (b, 0, i, 0, 0)),
                jax.ShapeDtypeStruct((B, nh, S // KT, VR, KT), BF16))

    gr = Hk * GATE_ROWS
    aug = (pl.BlockSpec((1, Hk, tm, LANES), lambda b, i: (b, 0, i, 0)), jax.ShapeDtypeStruct((B, Hk, S, LANES), BF16))
    outs = [dm(H), dm(H), aug, pm(Hk, BF16), dmt(Hk), dmt(Hk), pm(Hk, F32), pm(Hk, F32),
            (pl.BlockSpec((1, gr, tm), lambda b, i: (b, 0, i)), jax.ShapeDtypeStruct((B, gr, S), F32))]
    return pl.pallas_call(
        _nsa_prep_kernel,
        grid=(B, nt),
        in_specs=[pl.BlockSpec((tm, NP), lambda b, i: (b * nt + i, 0)),
                  pl.BlockSpec((tm, 1), lambda b, i: (b * nt + i, 0)),
                  _resident(inv.shape), _resident(qw.shape), _resident(ksw.shape), _resident(kww.shape),
                  _resident(bd.shape)],
        out_specs=[o[0] for o in outs],
        out_shape=[o[1] for o in outs],
        compiler_params=_cparams(("parallel", "parallel")),
        name="nsa_prep",
    )(proj, pos, inv, qw, ksw, kww, bd)


def _gelu_tanh(x):
    return 0.5 * x * (1.0 + jnp.tanh(np.float32(np.sqrt(2.0 / np.pi)) * (x + 0.044715 * (x * x * x))))


def _cmp_kernel(kc_ref, vc_ref, pe_ref, w1_ref, b1_ref, w2k_ref, b2k_ref, w2vt_ref, b2v_ref, knw_ref, kco_ref, vco_ref):
    nch = kc_ref.shape[2]

    def hidden(src, i):
        x = src[0, 0]
        p1 = _dot(_bf(x + pe_ref[i, 0:1]), w1_ref[i, 0])
        p2 = _dot(_bf(x + pe_ref[i, 1:2]), w1_ref[i, 1])
        return _bf(_gelu_tanh(p1 + pltpu.roll(p2, nch - 1, axis=0) + b1_ref[i]))

    kcc = _dot(hidden(kc_ref, 0), w2k_ref[...]) + b2k_ref[...]
    kco_ref[0, 0] = _bf(_rms(kcc, knw_ref[...]))
    vco_ref[0, 0] = _bf(_dot_nt(w2vt_ref[...], hidden(vc_ref, 1)) + b2v_ref[...])


def _nsa_compress(kc, vc, pe, w1, b1, w2k, b2k, w2vt, b2v, knw):
    B, Hk, nch, cw = kc.shape
    dh = NSA_DH
    spec = pl.BlockSpec((1, 1, nch, cw), lambda b, h: (b, h, 0, 0))
    return pl.pallas_call(
        _cmp_kernel,
        grid=(B, Hk),
        in_specs=[spec, spec] + [_resident(a.shape) for a in (pe, w1, b1, w2k, b2k, w2vt, b2v, knw)],
        out_specs=[pl.BlockSpec((1, 1, nch, dh), lambda b, h: (b, h, 0, 0)),
                   pl.BlockSpec((1, 1, dh, nch), lambda b, h: (b, h, 0, 0))],
        out_shape=[jax.ShapeDtypeStruct((B, Hk, nch, dh), BF16), jax.ShapeDtypeStruct((B, Hk, dh, nch), BF16)],
        compiler_params=_cparams(("parallel", "parallel")),
        name="nsa_compress",
    )(kc, vc, pe, w1, b1, w2k, b2k, w2vt, b2v, knw)


def _nsa_attn_kernel(qn_ref, qr_ref, kcc_ref, vcct_ref, ovlt_ref, ks_ref, vst_ref, kw_ref, vwt_ref, g_ref, o_ref,
                     selb_ref, p_ref, *, nsel):
    G, dh, tq = qn_ref.shape[2:]
    nck = kcc_ref.shape[2]
    ns = ovlt_ref.shape[0]
    tk = SEL_TK
    assert tk // SLC_BLOCK == 8
    s0 = pl.program_id(2) * tq
    tpos = s0 + lax.broadcasted_iota(jnp.int32, (1, tq), 1)
    heads = range(G)

    kcc = kcc_ref[0, 0]
    cend = lax.broadcasted_iota(jnp.int32, (nck, 1), 0) * CMP_STRIDE + (CMP_BLOCK - 1)
    cbias = jnp.where(cend <= tpos, 0.0, NEG)
    has_c = jnp.where(tpos >= CMP_BLOCK - 1, 1.0, 0.0)
    sc = [_dot(kcc, qn_ref[0, 0, g]) + cbias for g in heads]
    ec = [jnp.exp2(sc[g] - jnp.max(sc[g], axis=0, keepdims=True)) for g in heads]
    pc = [ec[g] * (has_c / jnp.sum(ec[g], axis=0, keepdims=True)) for g in heads]
    oc = [_dot(vcct_ref[0, 0], _bf(pc[g])) for g in heads]
    psum = pc[0]
    for g in range(1, G):
        psum = psum + pc[g]
    imp = _dot(ovlt_ref[...], _bf(psum))

    blk = lax.broadcasted_iota(jnp.int32, (ns, tq), 0)
    svalid = blk * SLC_BLOCK <= tpos
    dist = tpos // SLC_BLOCK - blk
    forced = (blk == 0) | ((dist >= 0) & (dist < N_LOCAL))
    score = jnp.where(svalid & forced, FORCE, jnp.where(svalid, imp, -1.0))
    ngrp = ns // 8
    selb = [[] for _ in range(ngrp)]
    for w in range(tq // LANES):
        sw_ = score[:, w * LANES:(w + 1) * LANES]
        sg = [sw_[8 * r:8 * r + 8] for r in range(ngrp)]
        bg = [blk[8 * r:8 * r + 8, 0:LANES] for r in range(ngrp)]
        rank = [jnp.zeros((8, LANES), F32) for _ in range(ngrp)]
        for j in range(ns):
            rowj = sw_[j:j + 1, :]
            for r in range(ngrp):
                if 8 * r + 7 <= j:
                    beat = jnp.where(rowj > sg[r], 1.0, 0.0)
                elif 8 * r > j:
                    beat = jnp.where(sg[r] > rowj, 0.0, 1.0)
                else:
                    beat = jnp.where(bg[r] > j, jnp.where(sg[r] > rowj, 0.0, 1.0),
                                     jnp.where(rowj > sg[r], 1.0, 0.0))
                rank[r] = rank[r] + beat
        for r in range(ngrp):
            selb[r].append(jnp.where((rank[r] < nsel) & (sg[r] >= 0.0), 0.0, NEG))
    zpad = jnp.zeros((8, tq), F32)
    for t in range(ngrp):
        selb_ref[t] = _bf(jnp.concatenate([jnp.concatenate(selb[t], axis=1), zpad], axis=0))

    nsub = tk // KT
    qr = [qr_ref[0, 0, g] for g in heads]
    qpad = jnp.zeros((LANES - dh - 16, tq), BF16)

    def tile_scores(kt):
        kaug = ks_ref[0, 0, pl.ds(pl.multiple_of(kt * tk, tk), tk), :]
        sb = selb_ref[kt]
        return tuple(_dot(kaug, jnp.concatenate([qr[g], sb, qpad], axis=0)) for g in heads)

    def step(carry, kt, bias):
        m, alpha, acc = carry
        s = tile_scores(kt)
        if bias is not None:
            s = [s[g] + bias for g in heads]
        kp = jnp.maximum(kt - 1, 0)
        vt = jnp.concatenate([vst_ref[0, 0, kp * nsub + j] for j in range(nsub)], axis=1)
        acc = [alpha[g] * acc[g] + _dot(vt, p_ref[g]) for g in heads]
        mn = [jnp.maximum(m[g], jnp.max(s[g], axis=0, keepdims=True)) for g in heads]
        for g in heads:
            p_ref[g] = _bf(jnp.exp2(s[g] - mn[g]))
        alpha = [jnp.exp2(m[g] - mn[g]) for g in heads]
        return tuple(mn), tuple(alpha), tuple(acc)

    p_ref[...] = jnp.zeros_like(p_ref)
    init = (tuple(jnp.full((1, tq), NEG, F32) for _ in heads),
            tuple(jnp.ones((1, tq), F32) for _ in heads),
            tuple(jnp.zeros((VR, tq), F32) for _ in heads))
    last = (s0 + tq - 1) // tk
    carry = lax.fori_loop(0, last, lambda kt, c: step(c, kt, None), init)
    kpos = last * tk + lax.broadcasted_iota(jnp.int32, (tk, 1), 0)
    _, alpha, acc = step(carry, last, jnp.where(kpos <= tpos, 0.0, NEG))
    vt = jnp.concatenate([vst_ref[0, 0, last * nsub + j] for j in range(nsub)], axis=1)
    acc = [alpha[g] * acc[g] + _dot(vt, p_ref[g]) for g in heads]
    osel = [acc[g][0:dh] * (1.0 / acc[g][dh:dh + 1]) for g in heads]

    ksub = lax.broadcasted_iota(jnp.int32, (KT, 1), 0)
    sw = [[] for _ in heads]
    vts = []
    for j in range((WINDOW + tq) // KT):
        start = s0 - WINDOW + j * KT
        st = pl.multiple_of(jnp.maximum(start, 0), KT)
        kpos = jnp.where(start >= 0, st, S_FAR) + ksub
        wbias = jnp.where((kpos <= tpos) & (kpos > tpos - WINDOW), 0.0, NEG)
        kj = kw_ref[0, 0, pl.ds(st, KT), :]
        vts.append(vwt_ref[0, 0, st // KT])
        for g in heads:
            sw[g].append(_dot(kj, qr[g]) + wbias)
    ow = []
    for g in heads:
        mw = sw[g][0].max(axis=0, keepdims=True)
        for j in range(1, len(vts)):
            mw = jnp.maximum(mw, sw[g][j].max(axis=0, keepdims=True))
        o = _dot(vts[0], _bf(jnp.exp2(sw[g][0] - mw)))
        for j in range(1, len(vts)):
            o = o + _dot(vts[j], _bf(jnp.exp2(sw[g][j] - mw)))
        ow.append(o[0:dh] * (1.0 / o[dh:dh + 1]))

    gt = g_ref[0]
    outs = [gt[3 * g:3 * g + 1] * oc[g] + gt[3 * g + 1:3 * g + 2] * osel[g] + gt[3 * g + 2:3 * g + 3] * ow[g]
            for g in heads]
    for c in range(G // 2):
        o_ref[:, c * LANES:(c + 1) * LANES] = _bf(jnp.concatenate([outs[2 * c], outs[2 * c + 1]], axis=0).T)


def _nsa_attn(qn, qr, kcc, vcct, ovlt, ks, vst, kw, vwt, gates, B, S, tq=256):
    Hk, G, dh = NSA_KV_HEADS, NSA_GROUP, NSA_DH
    nt = S // tq
    nck = kcc.shape[2]
    ns = ovlt.shape[0]
    q5 = lambda q: q.reshape(B, Hk, G, dh, S)
    qspec = pl.BlockSpec((1, 1, G, dh, tq), lambda b, h, i: (b, h, 0, 0, i))
    vspec = pl.BlockSpec((1, 1, S // KT, VR, KT), lambda b, h, i: (b, h, 0, 0, 0))
    return pl.pallas_call(
        functools.partial(_nsa_attn_kernel, nsel=min(SLC_TOPK, ns)),
        grid=(B, Hk, nt),
        in_specs=[qspec, qspec,
                  pl.BlockSpec((1, 1, nck, dh), lambda b, h, i: (b, h, 0, 0)),
                  pl.BlockSpec((1, 1, dh, nck), lambda b, h, i: (b, h, 0, 0)),
                  _resident(ovlt.shape),
                  pl.BlockSpec((1, 1, S, LANES), lambda b, h, i: (b, h, 0, 0)), vspec,
                  pl.BlockSpec((1, 1, S, dh), lambda b, h, i: (b, h, 0, 0)), vspec,
                  pl.BlockSpec((1, GATE_ROWS, tq), lambda b, h, i: (b, h, i))],
        out_specs=pl.BlockSpec((tq, G * dh), lambda b, h, i: (b * nt + i, h)),
        out_shape=jax.ShapeDtypeStruct((B * S, Hk * G * dh), BF16),
        scratch_shapes=[pltpu.VMEM((S // SEL_TK, 16, tq), BF16), pltpu.VMEM((G, SEL_TK, tq), BF16)],
        compiler_params=_cparams(("parallel", "parallel", "parallel")),
        name="nsa_attn",
    )(q5(qn), q5(qr), kcc, vcct, ovlt, ks, vst, kw, vwt, gates)


def _nsa_constants(S):
    dh = NSA_DH
    half = dh // 2
    inv = 1.0 / (ROPE_THETA ** (np.arange(0, dh, 2, dtype=np.float32) / dh))
    inv = np.tile(inv.astype(np.float32), LANES // half).reshape(1, LANES)
    lane = np.arange(LANES)
    bd = (lane[:, None] // dh == lane[None, :] // dh).astype(np.float32)
    nck = S // CMP_STRIDE
    ns = S // SLC_BLOCK
    ci = np.arange(nck)[None, :]
    sj = np.arange(ns)[:, None]
    ovlt = np.clip(np.minimum(ci * CMP_STRIDE + CMP_BLOCK, (sj + 1) * SLC_BLOCK)
                   - np.maximum(ci * CMP_STRIDE, sj * SLC_BLOCK), 0, None).astype(np.float32) / CMP_BLOCK
    return jnp.asarray(inv), jnp.asarray(bd, BF16), jnp.asarray(ovlt, BF16)


def _nsa_mixer(h2, nw, pos, w_in, q_norm, k_norm, cmp_pe, cmp_w1, cmp_b1, cmp_w2, cmp_b2, w_out, B, S):
    H, Hk, G, dh = NSA_HEADS, NSA_KV_HEADS, NSA_GROUP, NSA_DH
    inv, bd, ovlt = _nsa_constants(S)
    D = w_in.shape[0]
    wg = jnp.pad(w_in[:, _OFF_G:].reshape(D, Hk, 3 * G), ((0, 0), (0, 0), (0, GATE_ROWS - 3 * G)))
    wg = jnp.pad(wg.reshape(D, Hk * GATE_ROWS), ((0, 0), (0, LANES - Hk * GATE_ROWS)))
    w_in_p = jnp.concatenate([w_in[:, :_OFF_G], wg], axis=1).astype(BF16)
    proj = _norm_matmul(h2, nw, w_in_p, tn=w_in_p.shape[1] // 3)
    tile2 = lambda w: jnp.tile(w, LANES // dh).reshape(1, LANES)
    qn, qr, ks, kw, vst, vwt, kc, vc, gates = _nsa_prep(
        proj, pos.reshape(B * S, 1), inv, tile2(q_norm), tile2(k_norm[1]), tile2(k_norm[2]), bd, B, S)

    nch = S // CMP_STRIDE
    cw = CMP_STRIDE * dh
    kcc, vcct = _nsa_compress(
        kc.reshape(B, Hk, nch, cw), vc.reshape(B, Hk, nch, cw),
        cmp_pe.reshape(2, 2, cw), cmp_w1.reshape(2, 2, cw, CMP_HIDDEN).astype(BF16),
        cmp_b1.reshape(2, 1, CMP_HIDDEN), cmp_w2[0].astype(BF16), cmp_b2[0].reshape(1, dh),
        cmp_w2[1].T.astype(BF16), cmp_b2[1].reshape(dh, 1), k_norm[0].reshape(1, dh))

    return h2, (_nsa_attn(qn, qr, kcc, vcct, ovlt, ks, vst, kw, vwt, gates, B, S), w_out.astype(BF16))


def kernel(x, positions, ffn_norm, ffn_w_gate_up, ffn_w_down, mixer_norm, gdn_w_in, gdn_conv_w, gdn_A_log, gdn_dt_bias, gdn_out_norm, gdn_w_out, sc_w_in, sc_conv_w, sc_w_out, nsa_w_in, nsa_q_norm, nsa_k_norm, nsa_cmp_pe, nsa_cmp_w1, nsa_cmp_b1, nsa_cmp_w2, nsa_cmp_b2, nsa_w_out):
    B, S, D = x.shape
    depth = ffn_norm.shape[0]
    h = x.reshape(B * S, D)
    for layer in range(depth):
        h = _ffn(h, ffn_norm[layer, 0].reshape(1, D), *_prep_ffn_weights(ffn_w_gate_up[layer, 0], ffn_w_down[layer, 0]))
        nw = mixer_norm[layer].reshape(1, D)
        kind = layer % N_MIXERS
        j = layer // N_MIXERS
        proj = None
        if kind == 0:
            h, proj = _gdn_mixer(h, nw, gdn_w_in[j], gdn_conv_w[j], gdn_A_log[j], gdn_dt_bias[j],
                                 gdn_out_norm[j], gdn_w_out[j], B, S)
        elif kind == 1:
            h = _sc_mixer(h, nw, sc_w_in[j], sc_conv_w[j], sc_w_out[j], B, S)
        else:
            h, proj = _nsa_mixer(h, nw, positions, nsa_w_in[j], nsa_q_norm[j], nsa_k_norm[j], nsa_cmp_pe[j],
                                 nsa_cmp_w1[j], nsa_cmp_b1[j], nsa_cmp_w2[j], nsa_cmp_b2[j], nsa_w_out[j], B, S)
        h = _ffn(h, ffn_norm[layer, 1].reshape(1, D),
                 *_prep_ffn_weights(ffn_w_gate_up[layer, 1], ffn_w_down[layer, 1]), proj=proj)
    return h.reshape(B, S, D)
```

```python
import functools

import numpy as np
import jax
import jax.numpy as jnp
from jax import lax
from jax.experimental import pallas as pl
from jax.experimental.pallas import tpu as pltpu

F32 = jnp.float32
BF16 = jnp.bfloat16

EPS = 1e-6
NEG = -1e30
FORCE = 1e9
S_FAR = 1 << 30
LANES = 128

GDN_HEADS = 8
GDN_DK = 128
GDN_DV = 128
GDN_CONV = 4
GDN_CHUNK = 64
SC_WIDTH = 3
NSA_HEADS = 16
NSA_KV_HEADS = 4
NSA_GROUP = NSA_HEADS // NSA_KV_HEADS
NSA_DH = 64
CMP_BLOCK = 32
CMP_STRIDE = 16
CMP_HIDDEN = 256
SLC_BLOCK = 64
SLC_TOPK = 16
N_LOCAL = 2
WINDOW = 512
ROPE_THETA = 10000.0
N_MIXERS = 3
GATE_ROWS = 16

VMEM_LIMIT = 56 * 1024 * 1024


def _cparams(sem):
    return pltpu.CompilerParams(dimension_semantics=sem, vmem_limit_bytes=VMEM_LIMIT)


def _resident(shape):
    nd = len(shape)
    return pl.BlockSpec(shape, lambda *_: (0,) * nd, pipeline_mode=pl.Buffered(1))


def _rms(x, w):
    return x * lax.rsqrt(jnp.mean(x * x, axis=-1, keepdims=True) + EPS) * w


def _silu(x):
    h = 0.5 * x
    return h + h * jnp.tanh(h)


def _softplus(x):
    return jnp.maximum(x, 0.0) + jnp.log1p(jnp.exp(-jnp.abs(x)))


def _bf(x):
    return x.astype(BF16)


def _dot(a, b):
    return jnp.dot(a, b, preferred_element_type=F32)


def _dot_nt(a, b):
    return lax.dot_general(a, b, (((1,), (1,)), ((), ())), preferred_element_type=F32)


def _split(x):
    hi = x.astype(BF16)
    lo = (x - hi.astype(F32)).astype(BF16)
    return hi, lo


def _dot2_lhs(a, b_bf16):
    ah, al = _split(a)
    return _dot(ah, b_bf16) + _dot(al, b_bf16)


def _ffn_kernel(*refs, tf, with_proj):
    if with_proj:
        m_ref, wp_ref, x_ref, nw_ref, wg_ref, wu_ref, wd_ref, o_ref, a_ref = refs
        x = x_ref[...] + _dot(m_ref[...], wp_ref[...])
    else:
        x_ref, nw_ref, wg_ref, wu_ref, wd_ref, o_ref, a_ref = refs
        x = x_ref[...]
    xn = _bf(_rms(x, nw_ref[...]))
    for c in range(wg_ref.shape[1] // tf):
        cols = slice(c * tf, (c + 1) * tf)
        a_ref[:, cols] = _bf(_silu(_dot(xn, wg_ref[:, cols])) * _dot(xn, wu_ref[:, cols]))
    o_ref[...] = x + 0.5 * _dot(a_ref[...], wd_ref[...])


def _ffn(h2, nw, wg, wu, wd, proj=None, tm=512, tf=256):
    T, D = h2.shape
    row = lambda n: pl.BlockSpec((tm, n), lambda i: (i, 0))
    args, specs = [h2, nw, wg, wu, wd], [row(D)] + [_resident(a.shape) for a in (nw, wg, wu, wd)]
    if proj is not None:
        args = list(proj) + args
        specs = [row(proj[0].shape[1]), _resident(proj[1].shape)] + specs
    return pl.pallas_call(
        functools.partial(_ffn_kernel, tf=tf, with_proj=proj is not None),
        grid=(T // tm,),
        in_specs=specs,
        out_specs=row(D),
        out_shape=jax.ShapeDtypeStruct((T, D), F32),
        scratch_shapes=[pltpu.VMEM((tm, wg.shape[1]), BF16)],
        compiler_params=_cparams(("parallel",)),
        name="ffn",
    )(*args)


def _prep_ffn_weights(w_gate_up, w_down):
    fh = w_gate_up.shape[1] // 2
    return w_gate_up[:, :fh].astype(BF16), w_gate_up[:, fh:].astype(BF16), w_down.astype(BF16)


def _norm_matmul_kernel(x_ref, nw_ref, w_ref, o_ref, *, tn):
    xn = _rms(x_ref[...], nw_ref[...]).astype(BF16)
    for c in range(w_ref.shape[1] // tn):
        o_ref[:, c * tn:(c + 1) * tn] = _dot(xn, w_ref[:, c * tn:(c + 1) * tn])


def _norm_matmul(h2, nw, w, tn, tm=512):
    T, D = h2.shape
    N = w.shape[1]
    return pl.pallas_call(
        functools.partial(_norm_matmul_kernel, tn=tn),
        grid=(T // tm,),
        in_specs=[pl.BlockSpec((tm, D), lambda i: (i, 0)), _resident(nw.shape), _resident(w.shape)],
        out_specs=pl.BlockSpec((tm, N), lambda i: (i, 0)),
        out_shape=jax.ShapeDtypeStruct((T, N), F32),
        compiler_params=_cparams(("parallel",)),
        name="norm_matmul",
    )(h2, nw, w)


def _gdn_proj_kernel(x_ref, nw_ref, w_ref, cw_ref, o_ref, xbuf_ref, *, nq, tn):
    tm = x_ref.shape[0]
    n_out = w_ref.shape[1]

    @pl.when(pl.program_id(1) == 0)
    def _():
        xbuf_ref[0:8, :] = jnp.zeros((8, nq), F32)

    xn = _bf(_rms(x_ref[...], nw_ref[...]))
    for lo in range(0, n_out, tn):
        cols = slice(lo, min(lo + tn, n_out))
        raw = _dot(xn, w_ref[:, cols])
        if lo < nq:
            xbuf_ref[8:8 + tm, cols] = raw
            cw = cw_ref[:, cols]
            y = (xbuf_ref[5:5 + tm, cols] * cw[0:1] + xbuf_ref[6:6 + tm, cols] * cw[1:2]
                 + xbuf_ref[7:7 + tm, cols] * cw[2:3] + raw * cw[3:4])
            xbuf_ref[0:8, cols] = raw[tm - 8:tm]
            o_ref[:, cols] = _silu(y)
        else:
            o_ref[:, cols] = raw


def _gdn_proj(h2, nw, w, cw, B, S, tm=512, tn=512):
    T, D = h2.shape
    N = w.shape[1]
    nq = cw.shape[1]
    assert nq % tn == 0
    nt = S // tm
    return pl.pallas_call(
        functools.partial(_gdn_proj_kernel, nq=nq, tn=tn),
        grid=(B, nt),
        in_specs=[pl.BlockSpec((tm, D), lambda b, i: (b * nt + i, 0)),
                  _resident(nw.shape), _resident(w.shape), _resident(cw.shape)],
        out_specs=pl.BlockSpec((tm, N), lambda b, i: (b * nt + i, 0)),
        out_shape=jax.ShapeDtypeStruct((T, N), F32),
        scratch_shapes=[pltpu.VMEM((8 + tm, nq), F32)],
        compiler_params=_cparams(("parallel", "arbitrary")),
        name="gdn_proj",
    )(h2, nw, w, cw)


def _gdn_kernel(qkv_ref, gate_ref, ab_ref, alog_ref, dtb_ref, onw_ref, o_ref, s_ref, *, nch):
    C = GDN_CHUNK
    H, DK, DV = GDN_HEADS, GDN_DK, GDN_DV
    R = nch * C
    n = pl.program_id(1)

    @pl.when(n == 0)
    def _():
        s_ref[...] = jnp.zeros_like(s_ref)

    y = qkv_ref[...]

    ab = ab_ref[...]
    g_all = -jnp.exp(alog_ref[...]) * _softplus(ab + dtb_ref[...])
    beta_all = jax.nn.sigmoid(ab)
    rowc = lax.broadcasted_iota(jnp.int32, (R, LANES), 0) % C
    gc = g_all
    s = 1
    while s < C:
        gc = gc + jnp.where(rowc >= s, pltpu.roll(gc, s, axis=0), 0.0)
        s *= 2
    egc = jnp.exp(gc)
    gcc = [gc[c * C:(c + 1) * C] for c in range(nch)]
    gct = [g.T for g in gcc]
    glast = [g[C - 1:C, :] for g in gcc]
    ekd = [jnp.exp(glast[c] - gcc[c]) for c in range(nch)]
    egl = [jnp.exp(glast[c]) for c in range(nch)]

    ri = lax.broadcasted_iota(jnp.int32, (C, C), 0)
    ci = lax.broadcasted_iota(jnp.int32, (C, C), 1)
    lower = ri >= ci
    strict = ri > ci

    pairs = [(c, h) for c in range(nch) for h in range(H)]
    dec, qnb, knb, kbb, sol, qd, kdt = {}, {}, {}, {}, {}, {}, {}
    for p in pairs:
        c, h = p
        rows = slice(c * C, (c + 1) * C)
        gcol = gcc[c][:, h:h + 1]
        grow = gct[c][h:h + 1, :]
        beta = beta_all[rows, H + h:H + h + 1]
        e = egc[rows, h:h + 1]
        dec[p] = jnp.where(lower, jnp.exp(jnp.where(lower, gcol - grow, 0.0)), 0.0)
        q = y[rows, h * DK:(h + 1) * DK]
        k = y[rows, H * DK + h * DK:H * DK + (h + 1) * DK]
        v = y[rows, 2 * H * DK + h * DV:2 * H * DK + (h + 1) * DV]
        qn = q * lax.rsqrt(jnp.sum(q * q, -1, keepdims=True) + EPS) * (DK ** -0.5)
        kn = k * lax.rsqrt(jnp.sum(k * k, -1, keepdims=True) + EPS)
        kb = kn * beta
        qnb[p], knb[p], kbb[p] = _bf(qn), _bf(kn), _bf(kb)
        sol[p] = jnp.concatenate([v * beta, kb * e], axis=1)
        qd[p] = qn * e
        kdt[p] = _bf((kn * ekd[c][:, h:h + 1]).T)

    kk = {p: _dot_nt(kbb[p], knb[p]) for p in pairs}
    qk = {p: _dot_nt(qnb[p], knb[p]) for p in pairs}
    nil = {p: jnp.where(strict, -kk[p] * dec[p], 0.0) for p in pairs}
    attn = {p: _bf(jnp.where(lower, qk[p] * dec[p], 0.0)) for p in pairs}
    x = {p: _bf(nil[p]) for p in pairs}
    pw = 2
    while pw < C:
        x2 = {p: _dot(x[p], x[p]) for p in pairs}
        x = {p: _bf(x2[p]) for p in pairs}
        nil = {p: nil[p] + x2[p] + _dot(_bf(nil[p]), x[p]) for p in pairs}
        pw *= 2
    sol = {p: sol[p] + _dot(_bf(nil[p]), _bf(sol[p])) for p in pairs}

    st = [s_ref[h] for h in range(H)]
    for c in range(nch):
        r = [_dot(_bf(jnp.concatenate([sol[(c, h)][:, DV:], qd[(c, h)]], axis=0)), _bf(st[h])) for h in range(H)]
        vb = [_bf(sol[(c, h)][:, :DV] - r[h][:C]) for h in range(H)]
        o = [r[h][C:] + _dot(attn[(c, h)], vb[h]) for h in range(H)]
        st = [st[h] * egl[c][:, h:h + 1] + _dot(kdt[(c, h)], vb[h]) for h in range(H)]
        for h in range(H):
            gt = gate_ref[c * C:(c + 1) * C, h * DV:(h + 1) * DV]
            o_ref[c * C:(c + 1) * C, h * DV:(h + 1) * DV] = _bf(_rms(o[h], onw_ref[...]) * _silu(gt))
    for h in range(H):
        s_ref[h] = st[h]


def _gdn_core(proj, alog, dtb, onw, B, S, nch=4):
    R = nch * GDN_CHUNK
    NQ = 2 * GDN_HEADS * GDN_DK + GDN_HEADS * GDN_DV
    NG = GDN_HEADS * GDN_DV
    nstep = S // R
    T = B * S
    return pl.pallas_call(
        functools.partial(_gdn_kernel, nch=nch),
        grid=(B, nstep),
        in_specs=[pl.BlockSpec((R, NQ), lambda b, n: (b * nstep + n, 0)),
                  pl.BlockSpec((R, NG), lambda b, n: (b * nstep + n, NQ // NG)),
                  pl.BlockSpec((R, LANES), lambda b, n: (b * nstep + n, (NQ + NG) // LANES)),
                  _resident(alog.shape), _resident(dtb.shape), _resident(onw.shape)],
        out_specs=pl.BlockSpec((R, NG), lambda b, n: (b * nstep + n, 0)),
        out_shape=jax.ShapeDtypeStruct((T, NG), BF16),
        scratch_shapes=[pltpu.VMEM((GDN_HEADS, GDN_DK, GDN_DV), F32)],
        compiler_params=_cparams(("parallel", "arbitrary")),
        name="gdn_core",
    )(proj, proj, proj, alog, dtb, onw)


def _gdn_mixer(h2, nw, w_in, conv_w, a_log, dt_bias, out_norm, w_out, B, S):
    H = GDN_HEADS
    n_used = w_in.shape[1]
    n_pad = -n_used % LANES
    w_in_p = jnp.pad(w_in, ((0, 0), (0, n_pad))).astype(BF16)
    proj = _gdn_proj(h2, nw, w_in_p, conv_w, B, S)
    lane_pad = LANES - H
    alog = jnp.pad(a_log, (0, lane_pad)).reshape(1, LANES)
    dtb = jnp.pad(dt_bias, (0, lane_pad)).reshape(1, LANES)
    o = _gdn_core(proj, alog, dtb, out_norm.reshape(1, GDN_DV), B, S)
    return h2, (o, w_out.astype(BF16))


def _sc_kernel(h_ref, nw_ref, win_ref, cw_ref, wout_ref, o_ref, cx_ref):
    tm, D = h_ref.shape
    i = pl.program_id(1)

    @pl.when(i == 0)
    def _():
        cx_ref[0:8, :] = jnp.zeros((8, D), F32)

    x = h_ref[...]
    xn = _rms(x, nw_ref[...]).astype(BF16)
    bg = _dot(xn, win_ref[:, 0:D])
    cx = _dot(xn, win_ref[:, D:2 * D]) * _dot(xn, win_ref[:, 2 * D:3 * D])
    cx_ref[8:8 + tm, :] = cx
    cw = cw_ref[...]
    y = cx_ref[6:6 + tm, :] * cw[0:1] + cx_ref[7:7 + tm, :] * cw[1:2] + cx * cw[2:3]
    cx_ref[0:8, :] = cx[tm - 8:tm]
    o_ref[...] = x + _dot((bg * y).astype(BF16), wout_ref[...])


def _sc_mixer(h2, nw, w_in, conv_w, w_out, B, S, tm=512):
    T, D = h2.shape
    nt = S // tm
    return pl.pallas_call(
        _sc_kernel,
        grid=(B, nt),
        in_specs=[pl.BlockSpec((tm, D), lambda b, i: (b * nt + i, 0)),
                  _resident(nw.shape), _resident(w_in.shape), _resident(conv_w.shape), _resident(w_out.shape)],
        out_specs=pl.BlockSpec((tm, D), lambda b, i: (b * nt + i, 0)),
        out_shape=jax.ShapeDtypeStruct((T, D), F32),
        scratch_shapes=[pltpu.VMEM((8 + tm, D), F32)],
        compiler_params=_cparams(("parallel", "arbitrary")),
        name="short_conv",
    )(h2, nw, w_in.astype(BF16), conv_w, w_out.astype(BF16))


_NQ = NSA_HEADS * NSA_DH
_KVW = NSA_KV_HEADS * NSA_DH
_OFF_KC, _OFF_VC, _OFF_KS, _OFF_VS, _OFF_KW, _OFF_VW = (_NQ + i * _KVW for i in range(6))
_OFF_G = _NQ + 6 * _KVW
KT = 128
VR = NSA_DH + 16
SEL_TK = 512
LOG2E = float(np.log2(np.e))


def _half(slab, j):
    if j:
        slab = pltpu.roll(slab, NSA_DH, axis=1)
    return slab[:, 0:NSA_DH]


def _nsa_prep_kernel(p_ref, pos_ref, inv_ref, qw_ref, ksw_ref, kww_ref, bd_ref,
                     qn_ref, qr_ref, ks_ref, kw_ref, vs_ref, vw_ref, kc_ref, vc_ref, g_ref):
    tm = p_ref.shape[0]
    dh = NSA_DH
    ang = pos_ref[...].astype(F32) * inv_ref[...]
    cos = jnp.cos(ang)
    sin = jnp.sin(ang)
    lane = lax.broadcasted_iota(jnp.int32, (tm, LANES), 1)
    first_half = (lane % dh) < (dh // 2)
    bd = bd_ref[...]

    def seg_rms(x, w):
        ss = _dot2_lhs(x * x, bd)
        return x * lax.rsqrt(ss * (1.0 / dh) + EPS) * w

    def rope(x):
        rot = jnp.where(first_half, -pltpu.roll(x, LANES - dh // 2, axis=1), pltpu.roll(x, dh // 2, axis=1))
        return x * cos + rot * sin

    def put(ref, slab, c):
        for j in range(2):
            ref[0, 2 * c + j] = _half(slab, j).astype(ref.dtype)

    def put_t(ref, slab, c):
        t = slab.T
        for j in range(2):
            ref[0, 2 * c + j] = t[j * dh:(j + 1) * dh].astype(ref.dtype)

    def put_t_tiles(ref, slab, c):
        t = slab.T
        tail = jnp.where(lax.broadcasted_iota(jnp.int32, (VR - dh, tm), 0) == 0, 1.0, 0.0)
        for j in range(2):
            ext = jnp.concatenate([t[j * dh:(j + 1) * dh], tail], axis=0).astype(ref.dtype)
            for tt in range(tm // KT):
                ref[0, 2 * c + j, tt] = ext[:, tt * KT:(tt + 1) * KT]

    def put_aug(ref, slab, c):
        row = lax.broadcasted_iota(jnp.int32, (tm, LANES), 0)
        onehot = jnp.where(lane - dh == (row // SLC_BLOCK) % (SEL_TK // SLC_BLOCK), 1.0, 0.0)
        for j in range(2):
            kj = pltpu.roll(slab, dh, axis=1) if j else slab
            ref[0, 2 * c + j] = jnp.where(lane < dh, kj, onehot).astype(ref.dtype)

    scale = dh ** -0.5 * LOG2E
    for c in range(_NQ // LANES):
        xn = seg_rms(p_ref[:, c * LANES:(c + 1) * LANES], qw_ref[...])
        put_t(qn_ref, xn * scale, c)
        put_t(qr_ref, rope(xn) * scale, c)
    for c in range(_KVW // LANES):
        put_aug(ks_ref, rope(seg_rms(p_ref[:, _OFF_KS + c * LANES:_OFF_KS + (c + 1) * LANES], ksw_ref[...])), c)
        put(kw_ref, rope(seg_rms(p_ref[:, _OFF_KW + c * LANES:_OFF_KW + (c + 1) * LANES], kww_ref[...])), c)
        put_t_tiles(vs_ref, p_ref[:, _OFF_VS + c * LANES:_OFF_VS + (c + 1) * LANES], c)
        put_t_tiles(vw_ref, p_ref[:, _OFF_VW + c * LANES:_OFF_VW + (c + 1) * LANES], c)
        put(kc_ref, p_ref[:, _OFF_KC + c * LANES:_OFF_KC + (c + 1) * LANES], c)
        put(vc_ref, p_ref[:, _OFF_VC + c * LANES:_OFF_VC + (c + 1) * LANES], c)
    g_ref[0] = jax.nn.sigmoid(p_ref[:, _OFF_G:_OFF_G + LANES]).T[0:g_ref.shape[1]]


def _nsa_prep(proj, pos, inv, qw, ksw, kww, bd, B, S, tm=SEL_TK):
    H, Hk, dh = NSA_HEADS, NSA_KV_HEADS, NSA_DH
    nt = S // tm
    NP = proj.shape[1]

    def pm(nh, dt):
        return (pl.BlockSpec((1, nh, tm, dh), lambda b, i: (b, 0, i, 0)),
                jax.ShapeDtypeStruct((B, nh, S, dh), dt))

    def dm(nh):
        return (pl.BlockSpec((1, nh, dh, tm), lambda b, i: (b, 0, 0, i)),
                jax.ShapeDtypeStruct((B, nh, dh, S), BF16))

    def dmt(nh):
        return (pl.BlockSpec((1, nh, tm // KT, VR, KT), lambda b, i: (b, 0, i, 0, 0)),
                jax.ShapeDtypeStruct((B, nh, S // KT, VR, KT), BF16))

    gr = Hk * GATE_ROWS
    aug = (pl.BlockSpec((1, Hk, tm, LANES), lambda b, i: (b, 0, i, 0)), jax.ShapeDtypeStruct((B, Hk, S, LANES), BF16))
    outs = [dm(H), dm(H), aug, pm(Hk, BF16), dmt(Hk), dmt(Hk), pm(Hk, F32), pm(Hk, F32),
            (pl.BlockSpec((1, gr, tm), lambda b, i: (b, 0, i)), jax.ShapeDtypeStruct((B, gr, S), F32))]
    return pl.pallas_call(
        _nsa_prep_kernel,
        grid=(B, nt),
        in_specs=[pl.BlockSpec((tm, NP), lambda b, i: (b * nt + i, 0)),
                  pl.BlockSpec((tm, 1), lambda b, i: (b * nt + i, 0)),
                  _resident(inv.shape), _resident(qw.shape), _resident(ksw.shape), _resident(kww.shape),
                  _resident(bd.shape)],
        out_specs=[o[0] for o in outs],
        out_shape=[o[1] for o in outs],
        compiler_params=_cparams(("parallel", "parallel")),
        name="nsa_prep",
    )(proj, pos, inv, qw, ksw, kww, bd)


def _gelu_tanh(x):
    return 0.5 * x * (1.0 + jnp.tanh(np.float32(np.sqrt(2.0 / np.pi)) * (x + 0.044715 * (x * x * x))))


def _cmp_kernel(kc_ref, vc_ref, pe_ref, w1_ref, b1_ref, w2k_ref, b2k_ref, w2vt_ref, b2v_ref, knw_ref, kco_ref, vco_ref):
    nch = kc_ref.shape[2]

    def hidden(src, i):
        x = src[0, 0]
        p1 = _dot(_bf(x + pe_ref[i, 0:1]), w1_ref[i, 0])
        p2 = _dot(_bf(x + pe_ref[i, 1:2]), w1_ref[i, 1])
        return _bf(_gelu_tanh(p1 + pltpu.roll(p2, nch - 1, axis=0) + b1_ref[i]))

    kcc = _dot(hidden(kc_ref, 0), w2k_ref[...]) + b2k_ref[...]
    kco_ref[0, 0] = _bf(_rms(kcc, knw_ref[...]))
    vco_ref[0, 0] = _bf(_dot_nt(w2vt_ref[...], hidden(vc_ref, 1)) + b2v_ref[...])


def _nsa_compress(kc, vc, pe, w1, b1, w2k, b2k, w2vt, b2v, knw):
    B, Hk, nch, cw = kc.shape
    dh = NSA_DH
    spec = pl.BlockSpec((1, 1, nch, cw), lambda b, h: (b, h, 0, 0))
    return pl.pallas_call(
        _cmp_kernel,
        grid=(B, Hk),
        in_specs=[spec, spec] + [_resident(a.shape) for a in (pe, w1, b1, w2k, b2k, w2vt, b2v, knw)],
        out_specs=[pl.BlockSpec((1, 1, nch, dh), lambda b, h: (b, h, 0, 0)),
                   pl.BlockSpec((1, 1, dh, nch), lambda b, h: (b, h, 0, 0))],
        out_shape=[jax.ShapeDtypeStruct((B, Hk, nch, dh), BF16), jax.ShapeDtypeStruct((B, Hk, dh, nch), BF16)],
        compiler_params=_cparams(("parallel", "parallel")),
        name="nsa_compress",
    )(kc, vc, pe, w1, b1, w2k, b2k, w2vt, b2v, knw)


def _nsa_attn_kernel(qn_ref, qr_ref, kcc_ref, vcct_ref, ovlt_ref, ks_ref, vst_ref, kw_ref, vwt_ref, g_ref, o_ref,
                     selb_ref, p_ref, *, nsel):
    G, dh, tq = qn_ref.shape[2:]
    nck = kcc_ref.shape[2]
    ns = ovlt_ref.shape[0]
    tk = SEL_TK
    assert tk // SLC_BLOCK == 8
    s0 = pl.program_id(2) * tq
    tpos = s0 + lax.broadcasted_iota(jnp.int32, (1, tq), 1)
    heads = range(G)

    kcc = kcc_ref[0, 0]
    cend = lax.broadcasted_iota(jnp.int32, (nck, 1), 0) * CMP_STRIDE + (CMP_BLOCK - 1)
    cbias = jnp.where(cend <= tpos, 0.0, NEG)
    has_c = jnp.where(tpos >= CMP_BLOCK - 1, 1.0, 0.0)
    sc = [_dot(kcc, qn_ref[0, 0, g]) + cbias for g in heads]
    ec = [jnp.exp2(sc[g] - jnp.max(sc[g], axis=0, keepdims=True)) for g in heads]
    pc = [ec[g] * (has_c / jnp.sum(ec[g], axis=0, keepdims=True)) for g in heads]
    oc = [_dot(vcct_ref[0, 0], _bf(pc[g])) for g in heads]
    psum = pc[0]
    for g in range(1, G):
        psum = psum + pc[g]
    imp = _dot(ovlt_ref[...], _bf(psum))

    blk = lax.broadcasted_iota(jnp.int32, (ns, tq), 0)
    svalid = blk * SLC_BLOCK <= tpos
    dist = tpos // SLC_BLOCK - blk
    forced = (blk == 0) | ((dist >= 0) & (dist < N_LOCAL))
    score = jnp.where(svalid & forced, FORCE, jnp.where(svalid, imp, -1.0))
    ngrp = ns // 8
    selb = [[] for _ in range(ngrp)]
    for w in range(tq // LANES):
        sw_ = score[:, w * LANES:(w + 1) * LANES]
        sg = [sw_[8 * r:8 * r + 8] for r in range(ngrp)]
        bg = [blk[8 * r:8 * r + 8, 0:LANES] for r in range(ngrp)]
        rank = [jnp.zeros((8, LANES), F32) for _ in range(ngrp)]
        for j in range(ns):
            rowj = sw_[j:j + 1, :]
            for r in range(ngrp):
                if 8 * r + 7 <= j:
                    beat = jnp.where(rowj > sg[r], 1.0, 0.0)
                elif 8 * r > j:
                    beat = jnp.where(sg[r] > rowj, 0.0, 1.0)
                else:
                    beat = jnp.where(bg[r] > j, jnp.where(sg[r] > rowj, 0.0, 1.0),
                                     jnp.where(rowj > sg[r], 1.0, 0.0))
                rank[r] = rank[r] + beat
        for r in range(ngrp):
            selb[r].append(jnp.where((rank[r] < nsel) & (sg[r] >= 0.0), 0.0, NEG))
    zpad = jnp.zeros((8, tq), F32)
    for t in range(ngrp):
        selb_ref[t] = _bf(jnp.concatenate([jnp.concatenate(selb[t], axis=1), zpad], axis=0))

    nsub = tk // KT
    qr = [qr_ref[0, 0, g] for g in heads]
    qpad = jnp.zeros((LANES - dh - 16, tq), BF16)

    def tile_scores(kt):
        kaug = ks_ref[0, 0, pl.ds(pl.multiple_of(kt * tk, tk), tk), :]
        sb = selb_ref[kt]
        return tuple(_dot(kaug, jnp.concatenate([qr[g], sb, qpad], axis=0)) for g in heads)

    def step(carry, kt, bias):
        m, alpha, acc = carry
        s = tile_scores(kt)
        if bias is not None:
            s = [s[g] + bias for g in heads]
        kp = jnp.maximum(kt - 1, 0)
        vt = jnp.concatenate([vst_ref[0, 0, kp * nsub + j] for j in range(nsub)], axis=1)
        acc = [alpha[g] * acc[g] + _dot(vt, p_ref[g]) for g in heads]
        mn = [jnp.maximum(m[g], jnp.max(s[g], axis=0, keepdims=True)) for g in heads]
        for g in heads:
            p_ref[g] = _bf(jnp.exp2(s[g] - mn[g]))
        alpha = [jnp.exp2(m[g] - mn[g]) for g in heads]
        return tuple(mn), tuple(alpha), tuple(acc)

    p_ref[...] = jnp.zeros_like(p_ref)
    init = (tuple(jnp.full((1, tq), NEG, F32) for _ in heads),
            tuple(jnp.ones((1, tq), F32) for _ in heads),
            tuple(jnp.zeros((VR, tq), F32) for _ in heads))
    last = (s0 + tq - 1) // tk
    carry = lax.fori_loop(0, last, lambda kt, c: step(c, kt, None), init)
    kpos = last * tk + lax.broadcasted_iota(jnp.int32, (tk, 1), 0)
    _, alpha, acc = step(carry, last, jnp.where(kpos <= tpos, 0.0, NEG))
    vt = jnp.concatenate([vst_ref[0, 0, last * nsub + j] for j in range(nsub)], axis=1)
    acc = [alpha[g] * acc[g] + _dot(vt, p_ref[g]) for g in heads]
    osel = [acc[g][0:dh] * (1.0 / acc[g][dh:dh + 1]) for g in heads]

    ksub = lax.broadcasted_iota(jnp.int32, (KT, 1), 0)
    sw = [[] for _ in heads]
    vts = []
    for j in range((WINDOW + tq) // KT):
        start = s0 - WINDOW + j * KT
        st = pl.multiple_of(jnp.maximum(start, 0), KT)
        kpos = jnp.where(start >= 0, st, S_FAR) + ksub
        wbias = jnp.where((kpos <= tpos) & (kpos > tpos - WINDOW), 0.0, NEG)
        kj = kw_ref[0, 0, pl.ds(st, KT), :]
        vts.append(vwt_ref[0, 0, st // KT])
        for g in heads:
            sw[g].append(_dot(kj, qr[g]) + wbias)
    ow = []
    for g in heads:
        mw = sw[g][0].max(axis=0, keepdims=True)
        for j in range(1, len(vts)):
            mw = jnp.maximum(mw, sw[g][j].max(axis=0, keepdims=True))
        o = _dot(vts[0], _bf(jnp.exp2(sw[g][0] - mw)))
        for j in range(1, len(vts)):
            o = o + _dot(vts[j], _bf(jnp.exp2(sw[g][j] - mw)))
        ow.append(o[0:dh] * (1.0 / o[dh:dh + 1]))

    gt = g_ref[0]
    outs = [gt[3 * g:3 * g + 1] * oc[g] + gt[3 * g + 1:3 * g + 2] * osel[g] + gt[3 * g + 2:3 * g + 3] * ow[g]
            for g in heads]
    for c in range(G // 2):
        o_ref[:, c * LANES:(c + 1) * LANES] = _bf(jnp.concatenate([outs[2 * c], outs[2 * c + 1]], axis=0).T)


def _nsa_attn(qn, qr, kcc, vcct, ovlt, ks, vst, kw, vwt, gates, B, S, tq=256):
    Hk, G, dh = NSA_KV_HEADS, NSA_GROUP, NSA_DH
    nt = S // tq
    nck = kcc.shape[2]
    ns = ovlt.shape[0]
    q5 = lambda q: q.reshape(B, Hk, G, dh, S)
    qspec = pl.BlockSpec((1, 1, G, dh, tq), lambda b, h, i: (b, h, 0, 0, i))
    vspec = pl.BlockSpec((1, 1, S // KT, VR, KT), lambda b, h, i: (b, h, 0, 0, 0))
    return pl.pallas_call(
        functools.partial(_nsa_attn_kernel, nsel=min(SLC_TOPK, ns)),
        grid=(B, Hk, nt),
        in_specs=[qspec, qspec,
                  pl.BlockSpec((1, 1, nck, dh), lambda b, h, i: (b, h, 0, 0)),
                  pl.BlockSpec((1, 1, dh, nck), lambda b, h, i: (b, h, 0, 0)),
                  _resident(ovlt.shape),
                  pl.BlockSpec((1, 1, S, LANES), lambda b, h, i: (b, h, 0, 0)), vspec,
                  pl.BlockSpec((1, 1, S, dh), lambda b, h, i: (b, h, 0, 0)), vspec,
                  pl.BlockSpec((1, GATE_ROWS, tq), lambda b, h, i: (b, h, i))],
        out_specs=pl.BlockSpec((tq, G * dh), lambda b, h, i: (b * nt + i, h)),
        out_shape=jax.ShapeDtypeStruct((B * S, Hk * G * dh), BF16),
        scratch_shapes=[pltpu.VMEM((S // SEL_TK, 16, tq), BF16), pltpu.VMEM((G, SEL_TK, tq), BF16)],
        compiler_params=_cparams(("parallel", "parallel", "parallel")),
        name="nsa_attn",
    )(q5(qn), q5(qr), kcc, vcct, ovlt, ks, vst, kw, vwt, gates)


def _nsa_constants(S):
    dh = NSA_DH
    half = dh // 2
    inv = 1.0 / (ROPE_THETA ** (np.arange(0, dh, 2, dtype=np.float32) / dh))
    inv = np.tile(inv.astype(np.float32), LANES // half).reshape(1, LANES)
    lane = np.arange(LANES)
    bd = (lane[:, None] // dh == lane[None, :] // dh).astype(np.float32)
    nck = S // CMP_STRIDE
    ns = S // SLC_BLOCK
    ci = np.arange(nck)[None, :]
    sj = np.arange(ns)[:, None]
    ovlt = np.clip(np.minimum(ci * CMP_STRIDE + CMP_BLOCK, (sj + 1) * SLC_BLOCK)
                   - np.maximum(ci * CMP_STRIDE, sj * SLC_BLOCK), 0, None).astype(np.float32) / CMP_BLOCK
    return jnp.asarray(inv), jnp.asarray(bd, BF16), jnp.asarray(ovlt, BF16)


def _nsa_mixer(h2, nw, pos, w_in, q_norm, k_norm, cmp_pe, cmp_w1, cmp_b1, cmp_w2, cmp_b2, w_out, B, S):
    H, Hk, G, dh = NSA_HEADS, NSA_KV_HEADS, NSA_GROUP, NSA_DH
    inv, bd, ovlt = _nsa_constants(S)
    D = w_in.shape[0]
    wg = jnp.pad(w_in[:, _OFF_G:].reshape(D, Hk, 3 * G), ((0, 0), (0, 0), (0, GATE_ROWS - 3 * G)))
    wg = jnp.pad(wg.reshape(D, Hk * GATE_ROWS), ((0, 0), (0, LANES - Hk * GATE_ROWS)))
    w_in_p = jnp.concatenate([w_in[:, :_OFF_G], wg], axis=1).astype(BF16)
    proj = _norm_matmul(h2, nw, w_in_p, tn=w_in_p.shape[1] // 3)
    tile2 = lambda w: jnp.tile(w, LANES // dh).reshape(1, LANES)
    qn, qr, ks, kw, vst, vwt, kc, vc, gates = _nsa_prep(
        proj, pos.reshape(B * S, 1), inv, tile2(q_norm), tile2(k_norm[1]), tile2(k_norm[2]), bd, B, S)

    nch = S // CMP_STRIDE
    cw = CMP_STRIDE * dh
    kcc, vcct = _nsa_compress(
        kc.reshape(B, Hk, nch, cw), vc.reshape(B, Hk, nch, cw),
        cmp_pe.reshape(2, 2, cw), cmp_w1.reshape(2, 2, cw, CMP_HIDDEN).astype(BF16),
        cmp_b1.reshape(2, 1, CMP_HIDDEN), cmp_w2[0].astype(BF16), cmp_b2[0].reshape(1, dh),
        cmp_w2[1].T.astype(BF16), cmp_b2[1].reshape(dh, 1), k_norm[0].reshape(1, dh))

    return h2, (_nsa_attn(qn, qr, kcc, vcct, ovlt, ks, vst, kw, vwt, gates, B, S), w_out.astype(BF16))


def kernel(x, positions, ffn_norm, ffn_w_gate_up, ffn_w_down, mixer_norm, gdn_w_in, gdn_conv_w, gdn_A_log, gdn_dt_bias, gdn_out_norm, gdn_w_out, sc_w_in, sc_conv_w, sc_w_out, nsa_w_in, nsa_q_norm, nsa_k_norm, nsa_cmp_pe, nsa_cmp_w1, nsa_cmp_b1, nsa_cmp_w2, nsa_cmp_b2, nsa_w_out):
    B, S, D = x.shape
    depth = ffn_norm.shape[0]
    h = x.reshape(B * S, D)
    for layer in range(depth):
        h = _ffn(h, ffn_norm[layer, 0].reshape(1, D), *_prep_ffn_weights(ffn_w_gate_up[layer, 0], ffn_w_down[layer, 0]))
        nw = mixer_norm[layer].reshape(1, D)
        kind = layer % N_MIXERS
        j = layer // N_MIXERS
        proj = None
        if kind == 0:
            h, proj = _gdn_mixer(h, nw, gdn_w_in[j], gdn_conv_w[j], gdn_A_log[j], gdn_dt_bias[j],
                                 gdn_out_norm[j], gdn_w_out[j], B, S)
        elif kind == 1:
            h = _sc_mixer(h, nw, sc_w_in[j], sc_conv_w[j], sc_w_out[j], B, S)
        else:
            h, proj = _nsa_mixer(h, nw, positions, nsa_w_in[j], nsa_q_norm[j], nsa_k_norm[j], nsa_cmp_pe[j],
                                 nsa_cmp_w1[j], nsa_cmp_b1[j], nsa_cmp_w2[j], nsa_cmp_b2[j], nsa_w_out[j], B, S)
        h = _ffn(h, ffn_norm[layer, 1].reshape(1, D),
                 *_prep_ffn_weights(ffn_w_gate_up[layer, 1], ffn_w_down[layer, 1]), proj=proj)
    return h.reshape(B, S, D)
```

```python
import functools

import numpy as np
import jax
import jax.numpy as jnp
from jax import lax
from jax.experimental import pallas as pl
from jax.experimental.pallas import tpu as pltpu

F32 = jnp.float32
BF16 = jnp.bfloat16

EPS = 1e-6
NEG = -1e30
FORCE = 1e9
S_FAR = 1 << 30
LANES = 128

GDN_HEADS = 8
GDN_DK = 128
GDN_DV = 128
GDN_CONV = 4
GDN_CHUNK = 64
SC_WIDTH = 3
NSA_HEADS = 16
NSA_KV_HEADS = 4
NSA_GROUP = NSA_HEADS // NSA_KV_HEADS
NSA_DH = 64
CMP_BLOCK = 32
CMP_STRIDE = 16
CMP_HIDDEN = 256
SLC_BLOCK = 64
SLC_TOPK = 16
N_LOCAL = 2
WINDOW = 512
ROPE_THETA = 10000.0
N_MIXERS = 3
GATE_ROWS = 16

VMEM_LIMIT = 56 * 1024 * 1024


def _cparams(sem):
    return pltpu.CompilerParams(dimension_semantics=sem, vmem_limit_bytes=VMEM_LIMIT)


def _resident(shape):
    nd = len(shape)
    return pl.BlockSpec(shape, lambda *_: (0,) * nd, pipeline_mode=pl.Buffered(1))


def _rms(x, w):
    return x * lax.rsqrt(jnp.mean(x * x, axis=-1, keepdims=True) + EPS) * w


def _silu(x):
    h = 0.5 * x
    return h + h * jnp.tanh(h)


def _softplus(x):
    return jnp.maximum(x, 0.0) + jnp.log1p(jnp.exp(-jnp.abs(x)))


def _bf(x):
    return x.astype(BF16)


def _dot(a, b):
    return jnp.dot(a, b, preferred_element_type=F32)


def _dot_nt(a, b):
    return lax.dot_general(a, b, (((1,), (1,)), ((), ())), preferred_element_type=F32)


def _split(x):
    hi = x.astype(BF16)
    lo = (x - hi.astype(F32)).astype(BF16)
    return hi, lo


def _dot2_lhs(a, b_bf16):
    ah, al = _split(a)
    return _dot(ah, b_bf16) + _dot(al, b_bf16)


def _ffn_kernel(*refs, tf, with_proj):
    if with_proj:
        m_ref, wp_ref, x_ref, nw_ref, wg_ref, wu_ref, wd_ref, o_ref, a_ref = refs
        x = x_ref[...] + _dot(m_ref[...], wp_ref[...])
    else:
        x_ref, nw_ref, wg_ref, wu_ref, wd_ref, o_ref, a_ref = refs
        x = x_ref[...]
    xn = _bf(_rms(x, nw_ref[...]))
    for c in range(wg_ref.shape[1] // tf):
        cols = slice(c * tf, (c + 1) * tf)
        a_ref[:, cols] = _bf(_silu(_dot(xn, wg_ref[:, cols])) * _dot(xn, wu_ref[:, cols]))
    o_ref[...] = x + 0.5 * _dot(a_ref[...], wd_ref[...])


def _ffn(h2, nw, wg, wu, wd, proj=None, tm=512, tf=256):
    T, D = h2.shape
    row = lambda n: pl.BlockSpec((tm, n), lambda i: (i, 0))
    args, specs = [h2, nw, wg, wu, wd], [row(D)] + [_resident(a.shape) for a in (nw, wg, wu, wd)]
    if proj is not None:
        args = list(proj) + args
        specs = [row(proj[0].shape[1]), _resident(proj[1].shape)] + specs
    return pl.pallas_call(
        functools.partial(_ffn_kernel, tf=tf, with_proj=proj is not None),
        grid=(T // tm,),
        in_specs=specs,
        out_specs=row(D),
        out_shape=jax.ShapeDtypeStruct((T, D), F32),
        scratch_shapes=[pltpu.VMEM((tm, wg.shape[1]), BF16)],
        compiler_params=_cparams(("parallel",)),
        name="ffn",
    )(*args)


def _prep_ffn_weights(w_gate_up, w_down):
    fh = w_gate_up.shape[1] // 2
    return w_gate_up[:, :fh].astype(BF16), w_gate_up[:, fh:].astype(BF16), w_down.astype(BF16)


def _norm_matmul_kernel(x_ref, nw_ref, w_ref, o_ref, *, tn):
    xn = _rms(x_ref[...], nw_ref[...]).astype(BF16)
    for c in range(w_ref.shape[1] // tn):
        o_ref[:, c * tn:(c + 1) * tn] = _dot(xn, w_ref[:, c * tn:(c + 1) * tn])


def _norm_matmul(h2, nw, w, tn, tm=512):
    T, D = h2.shape
    N = w.shape[1]
    return pl.pallas_call(
        functools.partial(_norm_matmul_kernel, tn=tn),
        grid=(T // tm,),
        in_specs=[pl.BlockSpec((tm, D), lambda i: (i, 0)), _resident(nw.shape), _resident(w.shape)],
        out_specs=pl.BlockSpec((tm, N), lambda i: (i, 0)),
        out_shape=jax.ShapeDtypeStruct((T, N), F32),
        compiler_params=_cparams(("parallel",)),
        name="norm_matmul",
    )(h2, nw, w)


def _gdn_proj_kernel(x_ref, nw_ref, w_ref, cw_ref, o_ref, xbuf_ref, *, nq, tn):
    tm = x_ref.shape[0]
    n_out = w_ref.shape[1]

    @pl.when(pl.program_id(1) == 0)
    def _():
        xbuf_ref[...] = jnp.zeros((8, nq), F32)

    xn = _bf(_rms(x_ref[...], nw_ref[...]))
    for lo in range(0, n_out, tn):
        cols = slice(lo, min(lo + tn, n_out))
        raw = _dot(xn, w_ref[:, cols])
        if lo < nq:
            cw = cw_ref[:, cols]
            tail = xbuf_ref[:, cols]
            row8 = lax.broadcasted_iota(jnp.int32, tail.shape, 0)
            y = raw * cw[3:4]
            for sft in range(1, cw.shape[0]):
                sh = pltpu.roll(raw, sft, axis=0)
                head = jnp.where(row8 < sft, pltpu.roll(tail, sft, axis=0), sh[0:8])
                y = y + jnp.concatenate([head, sh[8:]], axis=0) * cw[3 - sft:4 - sft]
            xbuf_ref[:, cols] = raw[tm - 8:tm]
            o_ref[:, cols] = _silu(y)
        else:
            o_ref[:, cols] = raw


def _gdn_proj(h2, nw, w, cw, B, S, tm=512, tn=512):
    T, D = h2.shape
    N = w.shape[1]
    nq = cw.shape[1]
    assert nq % tn == 0
    nt = S // tm
    return pl.pallas_call(
        functools.partial(_gdn_proj_kernel, nq=nq, tn=tn),
        grid=(B, nt),
        in_specs=[pl.BlockSpec((tm, D), lambda b, i: (b * nt + i, 0)),
                  _resident(nw.shape), _resident(w.shape), _resident(cw.shape)],
        out_specs=pl.BlockSpec((tm, N), lambda b, i: (b * nt + i, 0)),
        out_shape=jax.ShapeDtypeStruct((T, N), F32),
        scratch_shapes=[pltpu.VMEM((8, nq), F32)],
        compiler_params=_cparams(("parallel", "arbitrary")),
        name="gdn_proj",
    )(h2, nw, w, cw)


def _gdn_kernel(qkv_ref, gate_ref, ab_ref, alog_ref, dtb_ref, onw_ref, o_ref, s_ref, *, nch):
    C = GDN_CHUNK
    H, DK, DV = GDN_HEADS, GDN_DK, GDN_DV
    R = nch * C
    n = pl.program_id(1)

    @pl.when(n == 0)
    def _():
        s_ref[...] = jnp.zeros_like(s_ref)

    y = qkv_ref[...]

    ab = ab_ref[...]
    g_all = -jnp.exp(alog_ref[...]) * _softplus(ab + dtb_ref[...])
    beta_all = jax.nn.sigmoid(ab)
    rowc = lax.broadcasted_iota(jnp.int32, (R, LANES), 0) % C
    gc = g_all
    s = 1
    while s < C:
        gc = gc + jnp.where(rowc >= s, pltpu.roll(gc, s, axis=0), 0.0)
        s *= 2
    egc = jnp.exp(gc)
    gcc = [gc[c * C:(c + 1) * C] for c in range(nch)]
    gct = [g.T for g in gcc]
    glast = [g[C - 1:C, :] for g in gcc]
    ekd = [jnp.exp(glast[c] - gcc[c]) for c in range(nch)]
    egl = [jnp.exp(glast[c]) for c in range(nch)]

    ri = lax.broadcasted_iota(jnp.int32, (C, C), 0)
    ci = lax.broadcasted_iota(jnp.int32, (C, C), 1)
    lower = ri >= ci
    strict = ri > ci

    pairs = [(c, h) for c in range(nch) for h in range(H)]
    dec, qnb, knb, kbb, sol, qd, kdt = {}, {}, {}, {}, {}, {}, {}
    for p in pairs:
        c, h = p
        rows = slice(c * C, (c + 1) * C)
        gcol = gcc[c][:, h:h + 1]
        grow = gct[c][h:h + 1, :]
        beta = beta_all[rows, H + h:H + h + 1]
        e = egc[rows, h:h + 1]
        dec[p] = jnp.where(lower, jnp.exp(jnp.where(lower, gcol - grow, 0.0)), 0.0)
        q = y[rows, h * DK:(h + 1) * DK]
        k = y[rows, H * DK + h * DK:H * DK + (h + 1) * DK]
        v = y[rows, 2 * H * DK + h * DV:2 * H * DK + (h + 1) * DV]
        qn = q * lax.rsqrt(jnp.sum(q * q, -1, keepdims=True) + EPS) * (DK ** -0.5)
        kn = k * lax.rsqrt(jnp.sum(k * k, -1, keepdims=True) + EPS)
        kb = kn * beta
        qnb[p], knb[p], kbb[p] = _bf(qn), _bf(kn), _bf(kb)
        sol[p] = jnp.concatenate([v * beta, kb * e], axis=1)
        qd[p] = qn * e
        kdt[p] = _bf((kn * ekd[c][:, h:h + 1]).T)

    kk = {p: _dot_nt(kbb[p], knb[p]) for p in pairs}
    qk = {p: _dot_nt(qnb[p], knb[p]) for p in pairs}
    nil = {p: jnp.where(strict, -kk[p] * dec[p], 0.0) for p in pairs}
    attn = {p: _bf(jnp.where(lower, qk[p] * dec[p], 0.0)) for p in pairs}
    x = {p: _bf(nil[p]) for p in pairs}
    pw = 2
    while pw < C:
        x2 = {p: _dot(x[p], x[p]) for p in pairs}
        x = {p: _bf(x2[p]) for p in pairs}
        nil = {p: nil[p] + x2[p] + _dot(_bf(nil[p]), x[p]) for p in pairs}
        pw *= 2
    sol = {p: sol[p] + _dot(_bf(nil[p]), _bf(sol[p])) for p in pairs}

    st = [s_ref[h] for h in range(H)]
    for c in range(nch):
        r = [_dot(_bf(jnp.concatenate([sol[(c, h)][:, DV:], qd[(c, h)]], axis=0)), _bf(st[h])) for h in range(H)]
        vb = [_bf(sol[(c, h)][:, :DV] - r[h][:C]) for h in range(H)]
        o = [r[h][C:] + _dot(attn[(c, h)], vb[h]) for h in range(H)]
        st = [st[h] * egl[c][:, h:h + 1] + _dot(kdt[(c, h)], vb[h]) for h in range(H)]
        for h in range(H):
            gt = gate_ref[c * C:(c + 1) * C, h * DV:(h + 1) * DV]
            o_ref[c * C:(c + 1) * C, h * DV:(h + 1) * DV] = _bf(_rms(o[h], onw_ref[...]) * _silu(gt))
    for h in range(H):
        s_ref[h] = st[h]


def _gdn_core(proj, alog, dtb, onw, B, S, nch=4):
    R = nch * GDN_CHUNK
    NQ = 2 * GDN_HEADS * GDN_DK + GDN_HEADS * GDN_DV
    NG = GDN_HEADS * GDN_DV
    nstep = S // R
    T = B * S
    return pl.pallas_call(
        functools.partial(_gdn_kernel, nch=nch),
        grid=(B, nstep),
        in_specs=[pl.BlockSpec((R, NQ), lambda b, n: (b * nstep + n, 0)),
                  pl.BlockSpec((R, NG), lambda b, n: (b * nstep + n, NQ // NG)),
                  pl.BlockSpec((R, LANES), lambda b, n: (b * nstep + n, (NQ + NG) // LANES)),
                  _resident(alog.shape), _resident(dtb.shape), _resident(onw.shape)],
        out_specs=pl.BlockSpec((R, NG), lambda b, n: (b * nstep + n, 0)),
        out_shape=jax.ShapeDtypeStruct((T, NG), BF16),
        scratch_shapes=[pltpu.VMEM((GDN_HEADS, GDN_DK, GDN_DV), F32)],
        compiler_params=_cparams(("parallel", "arbitrary")),
        name="gdn_core",
    )(proj, proj, proj, alog, dtb, onw)


def _gdn_mixer(h2, nw, w_in, conv_w, a_log, dt_bias, out_norm, w_out, B, S):
    H = GDN_HEADS
    n_used = w_in.shape[1]
    n_pad = -n_used % LANES
    w_in_p = jnp.pad(w_in, ((0, 0), (0, n_pad))).astype(BF16)
    proj = _gdn_proj(h2, nw, w_in_p, conv_w, B, S)
    lane_pad = LANES - H
    alog = jnp.pad(a_log, (0, lane_pad)).reshape(1, LANES)
    dtb = jnp.pad(dt_bias, (0, lane_pad)).reshape(1, LANES)
    o = _gdn_core(proj, alog, dtb, out_norm.reshape(1, GDN_DV), B, S)
    return h2, (o, w_out.astype(BF16))


def _sc_kernel(h_ref, nw_ref, win_ref, cw_ref, wout_ref, o_ref, cx_ref):
    tm, D = h_ref.shape
    i = pl.program_id(1)

    @pl.when(i == 0)
    def _():
        cx_ref[0:8, :] = jnp.zeros((8, D), F32)

    x = h_ref[...]
    xn = _rms(x, nw_ref[...]).astype(BF16)
    bg = _dot(xn, win_ref[:, 0:D])
    cx = _dot(xn, win_ref[:, D:2 * D]) * _dot(xn, win_ref[:, 2 * D:3 * D])
    cx_ref[8:8 + tm, :] = cx
    cw = cw_ref[...]
    y = cx_ref[6:6 + tm, :] * cw[0:1] + cx_ref[7:7 + tm, :] * cw[1:2] + cx * cw[2:3]
    cx_ref[0:8, :] = cx[tm - 8:tm]
    o_ref[...] = x + _dot((bg * y).astype(BF16), wout_ref[...])


def _sc_mixer(h2, nw, w_in, conv_w, w_out, B, S, tm=512):
    T, D = h2.shape
    nt = S // tm
    return pl.pallas_call(
        _sc_kernel,
        grid=(B, nt),
        in_specs=[pl.BlockSpec((tm, D), lambda b, i: (b * nt + i, 0)),
                  _resident(nw.shape), _resident(w_in.shape), _resident(conv_w.shape), _resident(w_out.shape)],
        out_specs=pl.BlockSpec((tm, D), lambda b, i: (b * nt + i, 0)),
        out_shape=jax.ShapeDtypeStruct((T, D), F32),
        scratch_shapes=[pltpu.VMEM((8 + tm, D), F32)],
        compiler_params=_cparams(("parallel", "arbitrary")),
        name="short_conv",
    )(h2, nw, w_in.astype(BF16), conv_w, w_out.astype(BF16))


_NQ = NSA_HEADS * NSA_DH
_KVW = NSA_KV_HEADS * NSA_DH
_OFF_KC, _OFF_VC, _OFF_KS, _OFF_VS, _OFF_KW, _OFF_VW = (_NQ + i * _KVW for i in range(6))
_OFF_G = _NQ + 6 * _KVW
KT = 128
VR = NSA_DH + 16
SEL_TK = 512
LOG2E = float(np.log2(np.e))


def _half(slab, j):
    if j:
        slab = pltpu.roll(slab, NSA_DH, axis=1)
    return slab[:, 0:NSA_DH]


def _nsa_prep_kernel(p_ref, pos_ref, inv_ref, qw_ref, ksw_ref, kww_ref, bd_ref,
                     qn_ref, qr_ref, ks_ref, kw_ref, vs_ref, vw_ref, kc_ref, vc_ref, g_ref):
    tm = p_ref.shape[0]
    dh = NSA_DH
    ang = pos_ref[...].astype(F32) * inv_ref[...]
    cos = jnp.cos(ang)
    sin = jnp.sin(ang)
    lane = lax.broadcasted_iota(jnp.int32, (tm, LANES), 1)
    first_half = (lane % dh) < (dh // 2)
    bd = bd_ref[...]

    def seg_rms(x, w):
        ss = _dot2_lhs(x * x, bd)
        return x * lax.rsqrt(ss * (1.0 / dh) + EPS) * w

    def rope(x):
        rot = jnp.where(first_half, -pltpu.roll(x, LANES - dh // 2, axis=1), pltpu.roll(x, dh // 2, axis=1))
        return x * cos + rot * sin

    def put(ref, slab, c):
        for j in range(2):
            ref[0, 2 * c + j] = _half(slab, j).astype(ref.dtype)

    def put_t(ref, slab, c):
        t = slab.T
        for j in range(2):
            ref[0, 2 * c + j] = t[j * dh:(j + 1) * dh].astype(ref.dtype)

    def put_t_tiles(ref, slab, c):
        t = slab.T
        tail = jnp.where(lax.broadcasted_iota(jnp.int32, (VR - dh, tm), 0) == 0, 1.0, 0.0)
        for j in range(2):
            ext = jnp.concatenate([t[j * dh:(j + 1) * dh], tail], axis=0).astype(ref.dtype)
            for tt in range(tm // KT):
                ref[0, 2 * c + j, tt] = ext[:, tt * KT:(tt + 1) * KT]

    def put_aug(ref, slab, c):
        row = lax.broadcasted_iota(jnp.int32, (tm, LANES), 0)
        onehot = jnp.where(lane - dh == (row // SLC_BLOCK) % (SEL_TK // SLC_BLOCK), 1.0, 0.0)
        for j in range(2):
            kj = pltpu.roll(slab, dh, axis=1) if j else slab
            ref[0, 2 * c + j] = jnp.where(lane < dh, kj, onehot).astype(ref.dtype)

    scale = dh ** -0.5 * LOG2E
    for c in range(_NQ // LANES):
        xn = seg_rms(p_ref[:, c * LANES:(c + 1) * LANES], qw_ref[...])
        put_t(qn_ref, xn * scale, c)
        put_t(qr_ref, rope(xn) * scale, c)
    for c in range(_KVW // LANES):
        put_aug(ks_ref, rope(seg_rms(p_ref[:, _OFF_KS + c * LANES:_OFF_KS + (c + 1) * LANES], ksw_ref[...])), c)
        put(kw_ref, rope(seg_rms(p_ref[:, _OFF_KW + c * LANES:_OFF_KW + (c + 1) * LANES], kww_ref[...])), c)
        put_t_tiles(vs_ref, p_ref[:, _OFF_VS + c * LANES:_OFF_VS + (c + 1) * LANES], c)
        put_t_tiles(vw_ref, p_ref[:, _OFF_VW + c * LANES:_OFF_VW + (c + 1) * LANES], c)
        put(kc_ref, p_ref[:, _OFF_KC + c * LANES:_OFF_KC + (c + 1) * LANES], c)
        put(vc_ref, p_ref[:, _OFF_VC + c * LANES:_OFF_VC + (c + 1) * LANES], c)
    g_ref[0] = jax.nn.sigmoid(p_ref[:, _OFF_G:_OFF_G + LANES]).T[0:g_ref.shape[1]]


def _nsa_prep(proj, pos, inv, qw, ksw, kww, bd, B, S, tm=SEL_TK):
    H, Hk, dh = NSA_HEADS, NSA_KV_HEADS, NSA_DH
    nt = S // tm
    NP = proj.shape[1]

    def pm(nh, dt):
        return (pl.BlockSpec((1, nh, tm, dh), lambda b, i: (b, 0, i, 0)),
                jax.ShapeDtypeStruct((B, nh, S, dh), dt))

    def dm(nh):
        return (pl.BlockSpec((1, nh, dh, tm), lambda b, i: (b, 0, 0, i)),
                jax.ShapeDtypeStruct((B, nh, dh, S), BF16))

    def dmt(nh):
        return (pl.BlockSpec((1, nh, tm // KT, VR, KT), lambda b, i: (b, 0, i, 0, 0)),
                jax.ShapeDtypeStruct((B, nh, S // KT, VR, KT), BF16))

    gr = Hk * GATE_ROWS
    aug = (pl.BlockSpec((1, Hk, tm, LANES), lambda b, i: (b, 0, i, 0)), jax.ShapeDtypeStruct((B, Hk, S, LANES), BF16))
    outs = [dm(H), dm(H), aug, pm(Hk, BF16), dmt(Hk), dmt(Hk), pm(Hk, F32), pm(Hk, F32),
            (pl.BlockSpec((1, gr, tm), lambda b, i: (b, 0, i)), jax.ShapeDtypeStruct((B, gr, S), F32))]
    return pl.pallas_call(
        _nsa_prep_kernel,
        grid=(B, nt),
        in_specs=[pl.BlockSpec((tm, NP), lambda b, i: (b * nt + i, 0)),
                  pl.BlockSpec((tm, 1), lambda b, i: (b * nt + i, 0)),
                  _resident(inv.shape), _resident(qw.shape), _resident(ksw.shape), _resident(kww.shape),
                  _resident(bd.shape)],
        out_specs=[o[0] for o in outs],
        out_shape=[o[1] for o in outs],
        compiler_params=_cparams(("parallel", "parallel")),
        name="nsa_prep",
    )(proj, pos, inv, qw, ksw, kww, bd)


def _gelu_tanh(x):
    return 0.5 * x * (1.0 + jnp.tanh(np.float32(np.sqrt(2.0 / np.pi)) * (x + 0.044715 * (x * x * x))))


def _cmp_kernel(kc_ref, vc_ref, pe_ref, w1_ref, b1_ref, w2k_ref, b2k_ref, w2vt_ref, b2v_ref, knw_ref, kco_ref, vco_ref):
    nch = kc_ref.shape[2]

    def hidden(src, i):
        x = src[0, 0]
        p1 = _dot(_bf(x + pe_ref[i, 0:1]), w1_ref[i, 0])
        p2 = _dot(_bf(x + pe_ref[i, 1:2]), w1_ref[i, 1])
        return _bf(_gelu_tanh(p1 + pltpu.roll(p2, nch - 1, axis=0) + b1_ref[i]))

    kcc = _dot(hidden(kc_ref, 0), w2k_ref[...]) + b2k_ref[...]
    kco_ref[0, 0] = _bf(_rms(kcc, knw_ref[...]))
    vco_ref[0, 0] = _bf(_dot_nt(w2vt_ref[...], hidden(vc_ref, 1)) + b2v_ref[...])


def _nsa_compress(kc, vc, pe, w1, b1, w2k, b2k, w2vt, b2v, knw):
    B, Hk, nch, cw = kc.shape
    dh = NSA_DH
    spec = pl.BlockSpec((1, 1, nch, cw), lambda b, h: (b, h, 0, 0))
    return pl.pallas_call(
        _cmp_kernel,
        grid=(B, Hk),
        in_specs=[spec, spec] + [_resident(a.shape) for a in (pe, w1, b1, w2k, b2k, w2vt, b2v, knw)],
        out_specs=[pl.BlockSpec((1, 1, nch, dh), lambda b, h: (b, h, 0, 0)),
                   pl.BlockSpec((1, 1, dh, nch), lambda b, h: (b, h, 0, 0))],
        out_shape=[jax.ShapeDtypeStruct((B, Hk, nch, dh), BF16), jax.ShapeDtypeStruct((B, Hk, dh, nch), BF16)],
        compiler_params=_cparams(("parallel", "parallel")),
        name="nsa_compress",
    )(kc, vc, pe, w1, b1, w2k, b2k, w2vt, b2v, knw)


def _nsa_attn_kernel(qn_ref, qr_ref, kcc_ref, vcct_ref, ovlt_ref, ks_ref, vst_ref, kw_ref, vwt_ref, g_ref, o_ref,
                     selb_ref, p_ref, *, nsel):
    G, dh, tq = qn_ref.shape[2:]
    nck = kcc_ref.shape[2]
    ns = ovlt_ref.shape[0]
    tk = SEL_TK
    assert tk // SLC_BLOCK == 8
    s0 = pl.program_id(2) * tq
    tpos = s0 + lax.broadcasted_iota(jnp.int32, (1, tq), 1)
    heads = range(G)

    kcc = kcc_ref[0, 0]
    cend = lax.broadcasted_iota(jnp.int32, (nck, 1), 0) * CMP_STRIDE + (CMP_BLOCK - 1)
    cbias = jnp.where(cend <= tpos, 0.0, NEG)
    has_c = jnp.where(tpos >= CMP_BLOCK - 1, 1.0, 0.0)
    sc = [_dot(kcc, qn_ref[0, 0, g]) + cbias for g in heads]
    ec = [jnp.exp2(sc[g] - jnp.max(sc[g], axis=0, keepdims=True)) for g in heads]
    pc = [ec[g] * (has_c / jnp.sum(ec[g], axis=0, keepdims=True)) for g in heads]
    oc = [_dot(vcct_ref[0, 0], _bf(pc[g])) for g in heads]
    psum = pc[0]
    for g in range(1, G):
        psum = psum + pc[g]
    imp = _dot(ovlt_ref[...], _bf(psum))

    blk = lax.broadcasted_iota(jnp.int32, (ns, tq), 0)
    svalid = blk * SLC_BLOCK <= tpos
    dist = tpos // SLC_BLOCK - blk
    forced = (blk == 0) | ((dist >= 0) & (dist < N_LOCAL))
    score = jnp.where(svalid & forced, FORCE, jnp.where(svalid, imp, -1.0))
    ngrp = ns // 8
    selb = [[] for _ in range(ngrp)]
    for w in range(tq // LANES):
        sw_ = score[:, w * LANES:(w + 1) * LANES]
        sg = [sw_[8 * r:8 * r + 8] for r in range(ngrp)]
        bg = [blk[8 * r:8 * r + 8, 0:LANES] for r in range(ngrp)]
        rank = [jnp.zeros((8, LANES), F32) for _ in range(ngrp)]
        for j in range(ns):
            rowj = sw_[j:j + 1, :]
            for r in range(ngrp):
                if 8 * r + 7 <= j:
                    beat = jnp.where(rowj > sg[r], 1.0, 0.0)
                elif 8 * r > j:
                    beat = jnp.where(sg[r] > rowj, 0.0, 1.0)
                else:
                    beat = jnp.where(bg[r] > j, jnp.where(sg[r] > rowj, 0.0, 1.0),
                                     jnp.where(rowj > sg[r], 1.0, 0.0))
                rank[r] = rank[r] + beat
        for r in range(ngrp):
            selb[r].append(jnp.where((rank[r] < nsel) & (sg[r] >= 0.0), 0.0, NEG))
    zpad = jnp.zeros((8, tq), F32)
    for t in range(ngrp):
        selb_ref[t] = _bf(jnp.concatenate([jnp.concatenate(selb[t], axis=1), zpad], axis=0))

    nsub = tk // KT
    qr = [qr_ref[0, 0, g] for g in heads]
    qpad = jnp.zeros((LANES - dh - 16, tq), BF16)

    def tile_scores(kt):
        kaug = ks_ref[0, 0, pl.ds(pl.multiple_of(kt * tk, tk), tk), :]
        sb = selb_ref[kt]
        return tuple(_dot(kaug, jnp.concatenate([qr[g], sb, qpad], axis=0)) for g in heads)

    def step(carry, kt, bias):
        m, alpha, acc = carry
        s = tile_scores(kt)
        if bias is not None:
            s = [s[g] + bias for g in heads]
        kp = jnp.maximum(kt - 1, 0)
        vt = jnp.concatenate([vst_ref[0, 0, kp * nsub + j] for j in range(nsub)], axis=1)
        acc = [alpha[g] * acc[g] + _dot(vt, p_ref[g]) for g in heads]
        mn = [jnp.maximum(m[g], jnp.max(s[g], axis=0, keepdims=True)) for g in heads]
        for g in heads:
            p_ref[g] = _bf(jnp.exp2(s[g] - mn[g]))
        alpha = [jnp.exp2(m[g] - mn[g]) for g in heads]
        return tuple(mn), tuple(alpha), tuple(acc)

    p_ref[...] = jnp.zeros_like(p_ref)
    init = (tuple(jnp.full((1, tq), NEG, F32) for _ in heads),
            tuple(jnp.ones((1, tq), F32) for _ in heads),
            tuple(jnp.zeros((VR, tq), F32) for _ in heads))
    last = (s0 + tq - 1) // tk
    carry = lax.fori_loop(0, last, lambda kt, c: step(c, kt, None), init)
    kpos = last * tk + lax.broadcasted_iota(jnp.int32, (tk, 1), 0)
    _, alpha, acc = step(carry, last, jnp.where(kpos <= tpos, 0.0, NEG))
    vt = jnp.concatenate([vst_ref[0, 0, last * nsub + j] for j in range(nsub)], axis=1)
    acc = [alpha[g] * acc[g] + _dot(vt, p_ref[g]) for g in heads]
    osel = [acc[g][0:dh] * (1.0 / acc[g][dh:dh + 1]) for g in heads]

    ksub = lax.broadcasted_iota(jnp.int32, (KT, 1), 0)
    sw = [[] for _ in heads]
    vts = []
    for j in range((WINDOW + tq) // KT):
        start = s0 - WINDOW + j * KT
        st = pl.multiple_of(jnp.maximum(start, 0), KT)
        kpos = jnp.where(start >= 0, st, S_FAR) + ksub
        wbias = jnp.where((kpos <= tpos) & (kpos > tpos - WINDOW), 0.0, NEG)
        kj = kw_ref[0, 0, pl.ds(st, KT), :]
        vts.append(vwt_ref[0, 0, st // KT])
        for g in heads:
            sw[g].append(_dot(kj, qr[g]) + wbias)
    ow = []
    for g in heads:
        mw = sw[g][0].max(axis=0, keepdims=True)
        for j in range(1, len(vts)):
            mw = jnp.maximum(mw, sw[g][j].max(axis=0, keepdims=True))
        o = _dot(vts[0], _bf(jnp.exp2(sw[g][0] - mw)))
        for j in range(1, len(vts)):
            o = o + _dot(vts[j], _bf(jnp.exp2(sw[g][j] - mw)))
        ow.append(o[0:dh] * (1.0 / o[dh:dh + 1]))

    gt = g_ref[0]
    outs = [gt[3 * g:3 * g + 1] * oc[g] + gt[3 * g + 1:3 * g + 2] * osel[g] + gt[3 * g + 2:3 * g + 3] * ow[g]
            for g in heads]
    for c in range(G // 2):
        o_ref[:, c * LANES:(c + 1) * LANES] = _bf(jnp.concatenate([outs[2 * c], outs[2 * c + 1]], axis=0).T)


def _nsa_attn(qn, qr, kcc, vcct, ovlt, ks, vst, kw, vwt, gates, B, S, tq=256):
    Hk, G, dh = NSA_KV_HEADS, NSA_GROUP, NSA_DH
    nt = S // tq
    nck = kcc.shape[2]
    ns = ovlt.shape[0]
    q5 = lambda q: q.reshape(B, Hk, G, dh, S)
    qspec = pl.BlockSpec((1, 1, G, dh, tq), lambda b, h, i: (b, h, 0, 0, i))
    vspec = pl.BlockSpec((1, 1, S // KT, VR, KT), lambda b, h, i: (b, h, 0, 0, 0))
    return pl.pallas_call(
        functools.partial(_nsa_attn_kernel, nsel=min(SLC_TOPK, ns)),
        grid=(B, Hk, nt),
        in_specs=[qspec, qspec,
                  pl.BlockSpec((1, 1, nck, dh), lambda b, h, i: (b, h, 0, 0)),
                  pl.BlockSpec((1, 1, dh, nck), lambda b, h, i: (b, h, 0, 0)),
                  _resident(ovlt.shape),
                  pl.BlockSpec((1, 1, S, LANES), lambda b, h, i: (b, h, 0, 0)), vspec,
                  pl.BlockSpec((1, 1, S, dh), lambda b, h, i: (b, h, 0, 0)), vspec,
                  pl.BlockSpec((1, GATE_ROWS, tq), lambda b, h, i: (b, h, i))],
        out_specs=pl.BlockSpec((tq, G * dh), lambda b, h, i: (b * nt + i, h)),
        out_shape=jax.ShapeDtypeStruct((B * S, Hk * G * dh), BF16),
        scratch_shapes=[pltpu.VMEM((S // SEL_TK, 16, tq), BF16), pltpu.VMEM((G, SEL_TK, tq), BF16)],
        compiler_params=_cparams(("parallel", "parallel", "parallel")),
        name="nsa_attn",
    )(q5(qn), q5(qr), kcc, vcct, ovlt, ks, vst, kw, vwt, gates)


def _nsa_constants(S):
    dh = NSA_DH
    half = dh // 2
    inv = 1.0 / (ROPE_THETA ** (np.arange(0, dh, 2, dtype=np.float32) / dh))
    inv = np.tile(inv.astype(np.float32), LANES // half).reshape(1, LANES)
    lane = np.arange(LANES)
    bd = (lane[:, None] // dh == lane[None, :] // dh).astype(np.float32)
    nck = S // CMP_STRIDE
    ns = S // SLC_BLOCK
    ci = np.arange(nck)[None, :]
    sj = np.arange(ns)[:, None]
    ovlt = np.clip(np.minimum(ci * CMP_STRIDE + CMP_BLOCK, (sj + 1) * SLC_BLOCK)
                   - np.maximum(ci * CMP_STRIDE, sj * SLC_BLOCK), 0, None).astype(np.float32) / CMP_BLOCK
    return jnp.asarray(inv), jnp.asarray(bd, BF16), jnp.asarray(ovlt, BF16)


def _nsa_mixer(h2, nw, pos, w_in, q_norm, k_norm, cmp_pe, cmp_w1, cmp_b1, cmp_w2, cmp_b2, w_out, B, S):
    H, Hk, G, dh = NSA_HEADS, NSA_KV_HEADS, NSA_GROUP, NSA_DH
    inv, bd, ovlt = _nsa_constants(S)
    D = w_in.shape[0]
    wg = jnp.pad(w_in[:, _OFF_G:].reshape(D, Hk, 3 * G), ((0, 0), (0, 0), (0, GATE_ROWS - 3 * G)))
    wg = jnp.pad(wg.reshape(D, Hk * GATE_ROWS), ((0, 0), (0, LANES - Hk * GATE_ROWS)))
    w_in_p = jnp.concatenate([w_in[:, :_OFF_G], wg], axis=1).astype(BF16)
    proj = _norm_matmul(h2, nw, w_in_p, tn=w_in_p.shape[1] // 3)
    tile2 = lambda w: jnp.tile(w, LANES // dh).reshape(1, LANES)
    qn, qr, ks, kw, vst, vwt, kc, vc, gates = _nsa_prep(
        proj, pos.reshape(B * S, 1), inv, tile2(q_norm), tile2(k_norm[1]), tile2(k_norm[2]), bd, B, S)

    nch = S // CMP_STRIDE
    cw = CMP_STRIDE * dh
    kcc, vcct = _nsa_compress(
        kc.reshape(B, Hk, nch, cw), vc.reshape(B, Hk, nch, cw),
        cmp_pe.reshape(2, 2, cw), cmp_w1.reshape(2, 2, cw, CMP_HIDDEN).astype(BF16),
        cmp_b1.reshape(2, 1, CMP_HIDDEN), cmp_w2[0].astype(BF16), cmp_b2[0].reshape(1, dh),
        cmp_w2[1].T.astype(BF16), cmp_b2[1].reshape(dh, 1), k_norm[0].reshape(1, dh))

    return h2, (_nsa_attn(qn, qr, kcc, vcct, ovlt, ks, vst, kw, vwt, gates, B, S), w_out.astype(BF16))


def kernel(x, positions, ffn_norm, ffn_w_gate_up, ffn_w_down, mixer_norm, gdn_w_in, gdn_conv_w, gdn_A_log, gdn_dt_bias, gdn_out_norm, gdn_w_out, sc_w_in, sc_conv_w, sc_w_out, nsa_w_in, nsa_q_norm, nsa_k_norm, nsa_cmp_pe, nsa_cmp_w1, nsa_cmp_b1, nsa_cmp_w2, nsa_cmp_b2, nsa_w_out):
    B, S, D = x.shape
    depth = ffn_norm.shape[0]
    h = x.reshape(B * S, D)
    for layer in range(depth):
        h = _ffn(h, ffn_norm[layer, 0].reshape(1, D), *_prep_ffn_weights(ffn_w_gate_up[layer, 0], ffn_w_down[layer, 0]))
        nw = mixer_norm[layer].reshape(1, D)
        kind = layer % N_MIXERS
        j = layer // N_MIXERS
        proj = None
        if kind == 0:
            h, proj = _gdn_mixer(h, nw, gdn_w_in[j], gdn_conv_w[j], gdn_A_log[j], gdn_dt_bias[j],
                                 gdn_out_norm[j], gdn_w_out[j], B, S)
        elif kind == 1:
            h = _sc_mixer(h, nw, sc_w_in[j], sc_conv_w[j], sc_w_out[j], B, S)
        else:
            h, proj = _nsa_mixer(h, nw, positions, nsa_w_in[j], nsa_q_norm[j], nsa_k_norm[j], nsa_cmp_pe[j],
                                 nsa_cmp_w1[j], nsa_cmp_b1[j], nsa_cmp_w2[j], nsa_cmp_b2[j], nsa_w_out[j], B, S)
        h = _ffn(h, ffn_norm[layer, 1].reshape(1, D),
                 *_prep_ffn_weights(ffn_w_gate_up[layer, 1], ffn_w_down[layer, 1]), proj=proj)
    return h.reshape(B, S, D)
```
